```python
import math
import jax, jax.numpy as jnp
from jax import lax
import numpy as np


D_MODEL = 2048
BATCH = 4
SEQ = 4096
DEPTH = 4

CHUNK = 64
Q_BLOCK = 128
D_FF = 5504
EPS = 1e-6

HG_HEADS = 6
HG_DK = 128
HG_DV = 128
HG_QK = HG_HEADS * HG_DK
HG_WIDTH = HG_HEADS * HG_DV
DF_HEADS = 4
DF_DQK = 64
DF_DV = 2 * DF_DQK
DF_QK = DF_HEADS * 2 * DF_DQK
DF_WIDTH = DF_HEADS * DF_DV
ML_HEADS = 6
ML_Q_RANK = 512
ML_KV_RANK = 256
ML_NOPE = 128
ML_ROPE = 64
ML_DV = 128
ML_WIDTH = ML_HEADS * ML_DV
ROPE_BASE = 10000.0
REL_BUCKETS = 32
REL_MAX_DIST = 128

D_MIX = HG_WIDTH + DF_WIDTH + ML_WIDTH

kernel_name = 'hybrid_hgrn2_diffattn_mla_macaron'


def _in_sizes():
    return (HG_QK, HG_QK, HG_WIDTH, HG_WIDTH,
            DF_QK, DF_QK, DF_WIDTH,
            ML_Q_RANK, ML_KV_RANK, ML_ROPE)


def _rms(x, g):
    xf = x.astype(jnp.float32)
    y = xf * lax.rsqrt(jnp.mean(xf * xf, axis=-1, keepdims=True) + EPS)
    return (y * g.astype(jnp.float32)).astype(x.dtype)


def _swiglu(h, w_gate, w_up, w_down):
    return (jax.nn.silu(h @ w_gate) * (h @ w_up)) @ w_down


def _chunk_mask(q_pos, k_pos):
    return (k_pos[None, :] // CHUNK) <= (q_pos[:, None] // CHUNK)


def _query_blocks(a):
    b, h, s, d = a.shape
    return a.reshape(b, h, s // Q_BLOCK, Q_BLOCK, d).transpose(2, 0, 1, 3, 4)


def _merge_blocks(o):
    n, b, h, qb, d = o.shape
    return o.transpose(1, 2, 0, 3, 4).reshape(b, h, n * qb, d)


def _t5_bucket(rel):
    half = REL_BUCKETS // 2
    max_exact = half // 2
    ret = (rel > 0).astype(jnp.int32) * half
    n = jnp.abs(rel)
    large = max_exact + (jnp.log(jnp.maximum(n, 1).astype(jnp.float32) / max_exact)
                         / math.log(REL_MAX_DIST / max_exact) * (half - max_exact)).astype(jnp.int32)
    large = jnp.minimum(large, half - 1)
    return ret + jnp.where(n < max_exact, n, large)


def _rope(x, pos):
    r = x.shape[-1]
    freqs = ROPE_BASE ** (-jnp.arange(0, r, 2, dtype=jnp.float32) / r)
    ang = pos.astype(jnp.float32)[:, None] * freqs[None, :]
    cos = jnp.cos(ang)[:, None, :]
    sin = jnp.sin(ang)[:, None, :]
    xf = x.astype(jnp.float32)
    x1, x2 = xf[..., : r // 2], xf[..., r // 2:]
    return jnp.concatenate([x1 * cos - x2 * sin, x2 * cos + x1 * sin], axis=-1).astype(x.dtype)


def _hgrn2(q, f_logit, i, g, lb, o_gain):
    b, s, _ = q.shape
    nc = s // CHUNK

    def heads_chunks(a, d):
        return a.reshape(b, nc, CHUNK, HG_HEADS, d).transpose(1, 0, 3, 2, 4).astype(jnp.float32)

    lbh = lb.astype(jnp.float32).reshape(HG_HEADS, 1, HG_DK)
    f = lbh + (1.0 - lbh) * jax.nn.sigmoid(heads_chunks(f_logit, HG_DK))
    log_f = jnp.log(f)
    k = 1.0 - f
    qc = heads_chunks(q, HG_DK) * (HG_DK ** -0.5)
    vc = heads_chunks(i, HG_DV)
    causal = jnp.tril(jnp.ones((CHUNK, CHUNK), dtype=bool))

    def step(state, inp):
        qt, kt, vt, lf = inp
        cum = jnp.cumsum(lf, axis=-2)
        o_inter = jnp.einsum('bhtk,bhkv->bhtv', qt * jnp.exp(cum), state)
        rel = cum[..., :, None, :] - cum[..., None, :, :]
        decay = jnp.exp(jnp.where(causal[:, :, None], rel, -jnp.inf))
        scores = jnp.einsum('bhtk,bhsk,bhtsk->bhts', qt, kt, decay)
        o = o_inter + jnp.einsum('bhts,bhsv->bhtv', scores, vt)
        last = cum[..., -1:, :]
        k_dec = kt * jnp.exp(last - cum)
        state = jnp.exp(last[..., 0, :])[..., None] * state + jnp.einsum('bhsk,bhsv->bhkv', k_dec, vt)
        return state, o

    s0 = jnp.zeros((b, HG_HEADS, HG_DK, HG_DV), jnp.float32)
    _, o = lax.scan(step, s0, (qc, k, vc, log_f))
    o = o.transpose(1, 0, 3, 2, 4).reshape(b, s, HG_HEADS, HG_DV)
    gh = g.reshape(b, s, HG_HEADS, HG_DV).astype(jnp.float32)
    o = _rms(o, o_gain) * jax.nn.silu(gh)
    return o.reshape(b, s, HG_WIDTH).astype(q.dtype)


def _diff_attn(q, k, v, q_gain, k_gain, lam, lambda_init, subln, rel_bias):
    b, s, _ = q.shape
    qh = _rms(q.reshape(b, s, DF_HEADS, 2, DF_DQK), q_gain).transpose(3, 0, 2, 1, 4)
    kh = _rms(k.reshape(b, s, DF_HEADS, 2, DF_DQK), k_gain).transpose(3, 0, 2, 1, 4)
    vh = v.reshape(b, s, DF_HEADS, DF_DV).transpose(0, 2, 1, 3)
    k1, k2 = kh[0], kh[1]
    k_pos = jnp.arange(s)
    scale = DF_DQK ** -0.5
    table = rel_bias.astype(jnp.float32)

    def blk(args):
        idx, a1, a2 = args
        q_pos = idx * Q_BLOCK + jnp.arange(Q_BLOCK)
        mask = _chunk_mask(q_pos, k_pos)
        bias = table[_t5_bucket(k_pos[None, :] - q_pos[:, None])].transpose(2, 0, 1)

        def probs(a, kk):
            logits = jnp.einsum('bhqd,bhkd->bhqk', a, kk).astype(jnp.float32) * scale + bias
            return jax.nn.softmax(jnp.where(mask, logits, -jnp.inf), axis=-1)

        w = probs(a1, k1) - lam * probs(a2, k2)
        return jnp.einsum('bhqk,bhkd->bhqd', w.astype(vh.dtype), vh)

    nqb = s // Q_BLOCK
    o = _merge_blocks(lax.map(blk, (jnp.arange(nqb), _query_blocks(qh[0]), _query_blocks(qh[1]))))
    o = _rms(o, subln) * (1.0 - lambda_init)
    return o.transpose(0, 2, 1, 3).reshape(b, s, DF_WIDTH).astype(q.dtype)


def _mla(c_q, c_kv, k_rope, q_lora_norm, w_uq, kv_lora_norm, w_ukv, q_gain, k_gain):
    b, s, _ = c_q.shape
    pos = jnp.arange(s)
    q = (_rms(c_q, q_lora_norm) @ w_uq).reshape(b, s, ML_HEADS, ML_NOPE + ML_ROPE)
    kv = (_rms(c_kv, kv_lora_norm) @ w_ukv).reshape(b, s, ML_HEADS, ML_NOPE + ML_DV)
    k_nope, v = kv[..., :ML_NOPE], kv[..., ML_NOPE:]
    k = jnp.concatenate([k_nope, jnp.broadcast_to(k_rope[:, :, None, :], (b, s, ML_HEADS, ML_ROPE))], axis=-1)
    q = _rms(q, q_gain)
    k = _rms(k, k_gain)
    q = jnp.concatenate([q[..., :ML_NOPE], _rope(q[..., ML_NOPE:], pos)], axis=-1).transpose(0, 2, 1, 3)
    k = jnp.concatenate([k[..., :ML_NOPE], _rope(k[..., ML_NOPE:], pos)], axis=-1).transpose(0, 2, 1, 3)
    v = v.transpose(0, 2, 1, 3)
    scale = (ML_NOPE + ML_ROPE) ** -0.5

    def blk(args):
        idx, qb = args
        q_pos = idx * Q_BLOCK + jnp.arange(Q_BLOCK)
        mask = _chunk_mask(q_pos, pos)
        logits = jnp.einsum('bhqd,bhkd->bhqk', qb, k).astype(jnp.float32) * scale
        p = jax.nn.softmax(jnp.where(mask, logits, -jnp.inf), axis=-1)
        return jnp.einsum('bhqk,bhkd->bhqd', p.astype(v.dtype), v)

    o = _merge_blocks(lax.map(blk, (jnp.arange(s // Q_BLOCK), _query_blocks(q))))
    return o.transpose(0, 2, 1, 3).reshape(b, s, ML_WIDTH).astype(c_q.dtype)


def setup_inputs(seed: int = 0) -> dict:
    key = jax.random.key(seed)
    ks = iter(jax.random.split(key, 40))
    L, D, F = DEPTH, D_MODEL, D_FF
    p_in = sum(_in_sizes())

    def w(shape, fan_in):
        return jax.random.normal(next(ks), shape, jnp.float32) * fan_in ** -0.5

    def gain(shape):
        return 1.0 + 0.02 * jax.random.normal(next(ks), shape, jnp.float32)

    def small(shape, sc):
        return sc * jax.random.normal(next(ks), shape, jnp.float32)

    return {
        'x': jax.random.normal(next(ks), (BATCH, SEQ, D), jnp.float32),
        'ffn_a_norm': gain((L, D)),
        'ffn_a_w_gate': w((L, D, F), D),
        'ffn_a_w_up': w((L, D, F), D),
        'ffn_a_w_down': w((L, F, D), F),
        'mix_norm': gain((L, D)),
        'w_in': w((L, D, p_in), D),
        'w_out': w((L, D_MIX, D), D_MIX),
        'hgrn_lb_logits': small((L, HG_QK), 1.0),
        'hgrn_out_norm': gain((L, HG_DV)),
        'diff_q_norm': gain((L, DF_DQK)),
        'diff_k_norm': gain((L, DF_DQK)),
        'diff_lambda_q1': small((L, DF_DQK), 0.1),
        'diff_lambda_k1': small((L, DF_DQK), 0.1),
        'diff_lambda_q2': small((L, DF_DQK), 0.1),
        'diff_lambda_k2': small((L, DF_DQK), 0.1),
        'diff_subln': gain((L, DF_DV)),
        'rel_bias': small((REL_BUCKETS, DF_HEADS), 0.5),
        'mla_q_lora_norm': gain((L, ML_Q_RANK)),
        'mla_w_uq': w((L, ML_Q_RANK, ML_HEADS * (ML_NOPE + ML_ROPE)), ML_Q_RANK),
        'mla_kv_lora_norm': gain((L, ML_KV_RANK)),
        'mla_w_ukv': w((L, ML_KV_RANK, ML_HEADS * (ML_NOPE + ML_DV)), ML_KV_RANK),
        'mla_q_norm': gain((L, ML_NOPE + ML_ROPE)),
        'mla_k_norm': gain((L, ML_NOPE + ML_ROPE)),
        'ffn_b_norm': gain((L, D)),
        'ffn_b_w_gate': w((L, D, F), D),
        'ffn_b_w_up': w((L, D, F), D),
        'ffn_b_w_down': w((L, F, D), F),
    }


def reference(x, ffn_a_norm, ffn_a_w_gate, ffn_a_w_up, ffn_a_w_down, mix_norm, w_in, w_out,
              hgrn_lb_logits, hgrn_out_norm, diff_q_norm, diff_k_norm, diff_lambda_q1, diff_lambda_k1,
              diff_lambda_q2, diff_lambda_k2, diff_subln, rel_bias, mla_q_lora_norm, mla_w_uq,
              mla_kv_lora_norm, mla_w_ukv, mla_q_norm, mla_k_norm, ffn_b_norm, ffn_b_w_gate,
              ffn_b_w_up, ffn_b_w_down):
    lb_all = jnp.cumsum(jax.nn.softmax(hgrn_lb_logits.astype(jnp.float32), axis=0), axis=0)
    lb_all = lb_all - lb_all[0:1]
    sizes = _in_sizes()
    offs = [0]
    for sz in sizes:
        offs.append(offs[-1] + sz)

    for l in range(DEPTH):
        x = x + 0.5 * _swiglu(_rms(x, ffn_a_norm[l]), ffn_a_w_gate[l], ffn_a_w_up[l], ffn_a_w_down[l])
        proj = _rms(x, mix_norm[l]) @ w_in[l]
        p = [proj[..., offs[j]:offs[j + 1]] for j in range(len(sizes))]
        o_hg = _hgrn2(p[0], p[1], p[2], p[3], lb_all[l], hgrn_out_norm[l])
        lambda_init = 0.8 - 0.6 * math.exp(-0.3 * l)
        lam = (jnp.exp(jnp.sum(diff_lambda_q1[l].astype(jnp.float32) * diff_lambda_k1[l].astype(jnp.float32)))
               - jnp.exp(jnp.sum(diff_lambda_q2[l].astype(jnp.float32) * diff_lambda_k2[l].astype(jnp.float32)))
               + lambda_init)
        o_df = _diff_attn(p[4], p[5], p[6], diff_q_norm[l], diff_k_norm[l], lam, lambda_init,
                          diff_subln[l], rel_bias)
        o_ml = _mla(p[7], p[8], p[9], mla_q_lora_norm[l], mla_w_uq[l], mla_kv_lora_norm[l],
                    mla_w_ukv[l], mla_q_norm[l], mla_k_norm[l])
        x = x + (jnp.concatenate([o_hg, o_df, o_ml], axis=-1) @ w_out[l]).astype(x.dtype)
        x = x + 0.5 * _swiglu(_rms(x, ffn_b_norm[l]), ffn_b_w_gate[l], ffn_b_w_up[l], ffn_b_w_down[l])
    return x
```

```python
import functools
import math

import jax
import jax.numpy as jnp
import numpy as np
from jax import lax
from jax.experimental import pallas as pl
from jax.experimental.pallas import tpu as pltpu

F32 = jnp.float32
BF16 = jnp.bfloat16

D_MODEL = 2048
DEPTH = 4
CHUNK = 64
D_FF = 5504
EPS = 1e-6
HG_HEADS = 6
HG_DK = 128
HG_DV = 128
HG_W = HG_HEADS * HG_DK
DF_HEADS = 4
DF_DQK = 64
DF_DV = 128
DF_W = DF_HEADS * DF_DV
ML_HEADS = 6
ML_Q_RANK = 512
ML_KV_RANK = 256
ML_NOPE = 128
ML_ROPE = 64
ML_DV = 128
ML_DQK = ML_NOPE + ML_ROPE
ML_W = ML_HEADS * ML_DV
ROPE_BASE = 10000.0
REL_BUCKETS = 32
REL_MAX_DIST = 128
P_IN = 4 * HG_W + 3 * DF_W + ML_Q_RANK + ML_KV_RANK + ML_ROPE

LANES = 128
ML_HEAD_PAD = 2 * LANES
FF_TILE = 512
D_FF_PAD = -(-D_FF // FF_TILE) * FF_TILE
P_IN_PAD = -(-P_IN // FF_TILE) * FF_TILE
ROW_TILE = 512
HG_ROWS = 256
ATT_TILE = 256
NEG_BIG = -1e30
VMEM_LIMIT = 56 * 1024 * 1024

_HG_COL = 0
_DF_COL = (4 * HG_W) // DF_W
_MLQ_COL = (4 * HG_W + 3 * DF_W) // ML_Q_RANK
_MLKV_COL = (4 * HG_W + 3 * DF_W + ML_Q_RANK) // ML_KV_RANK
_MLR_COL = (4 * HG_W + 3 * DF_W + ML_Q_RANK + ML_KV_RANK) // LANES


def _dot(a, b):
    return jnp.dot(a, b, preferred_element_type=F32)


def _dot_nt(a, b):
    return lax.dot_general(a, b, (((1,), (1,)), ((), ())), preferred_element_type=F32)


def _dot_tn(a, b):
    return lax.dot_general(a, b, (((0,), (0,)), ((), ())), preferred_element_type=F32)


def _params(*sem):
    return pltpu.CompilerParams(dimension_semantics=sem, vmem_limit_bytes=VMEM_LIMIT)


def _rms_rows(x, gain):
    return x * lax.rsqrt(jnp.mean(x * x, axis=-1, keepdims=True) + EPS) * gain


def _ffn_kernel(x_ref, g_ref, wg_ref, wu_ref, wd_ref, o_ref, h_ref):
    @pl.when(pl.program_id(1) == 0)
    def _():
        x = x_ref[...]
        h_ref[...] = _rms_rows(x, g_ref[...]).astype(BF16)
        o_ref[...] = x

    h = h_ref[...]
    a = _dot(h, wg_ref[...])
    u = _dot(h, wu_ref[...])
    act = (0.5 * a) * jax.nn.sigmoid(a) * u
    o_ref[...] += _dot(act.astype(BF16), wd_ref[...])


def _ffn(x, gains, wg, wu, wd, layer):
    m, d = x.shape
    fp = wg.shape[-1]
    return pl.pallas_call(
        _ffn_kernel,
        grid=(m // ROW_TILE, fp // FF_TILE),
        in_specs=[
            pl.BlockSpec((ROW_TILE, d), lambda i, j: (i, 0)),
            pl.BlockSpec((None, 1, d), lambda i, j: (layer, 0, 0)),
            pl.BlockSpec((None, d, FF_TILE), lambda i, j: (layer, 0, j)),
            pl.BlockSpec((None, d, FF_TILE), lambda i, j: (layer, 0, j)),
            pl.BlockSpec((None, FF_TILE, d), lambda i, j: (layer, j, 0)),
        ],
        out_specs=pl.BlockSpec((ROW_TILE, d), lambda i, j: (i, 0)),
        out_shape=jax.ShapeDtypeStruct((m, d), F32),
        scratch_shapes=[pltpu.VMEM((ROW_TILE, d), BF16)],
        compiler_params=_params("parallel", "arbitrary"),
        name="ffn",
    )(x, gains, wg, wu, wd)


def _proj_kernel(x_ref, g_ref, w_ref, o_ref, h_ref):
    @pl.when(pl.program_id(1) == 0)
    def _():
        h_ref[...] = _rms_rows(x_ref[...], g_ref[...]).astype(BF16)

    o_ref[...] = _dot(h_ref[...], w_ref[...])


def _proj(x, gains, w, layer):
    m, d = x.shape
    n = w.shape[-1]
    return pl.pallas_call(
        _proj_kernel,
        grid=(m // ROW_TILE, n // FF_TILE),
        in_specs=[
            pl.BlockSpec((ROW_TILE, d), lambda i, j: (i, 0)),
            pl.BlockSpec((None, 1, d), lambda i, j: (layer, 0, 0)),
            pl.BlockSpec((None, d, FF_TILE), lambda i, j: (layer, 0, j)),
        ],
        out_specs=pl.BlockSpec((ROW_TILE, FF_TILE), lambda i, j: (i, j)),
        out_shape=jax.ShapeDtypeStruct((m, n), F32),
        scratch_shapes=[pltpu.VMEM((ROW_TILE, d), BF16)],
        compiler_params=_params("parallel", "arbitrary"),
        name="proj",
    )(x, gains, w)


def _hgrn_level_masks():
    t = np.arange(CHUNK)[:, None]
    s = np.arange(CHUNK)[None, :]
    levels = [(((t >> l) ^ (s >> l)) == 1) & (t > s) for l in range(6)]
    levels.append(t == s)
    return np.stack(levels).astype(np.float32)


def _block_bcast(p, level, row):
    n, w = p.shape
    half = 1 << level
    size = 2 * half
    if level == 0:
        return jnp.where((row & 1) == 0, p, pltpu.roll(p, 1, 0))
    if level == 1:
        r = row & 3
        return jnp.where(r == 0, pltpu.roll(p, n - 1, 0),
                         jnp.where(r == 1, p,
                                   jnp.where(r == 2, pltpu.roll(p, 1, 0), pltpu.roll(p, 2, 0))))
    pieces = []
    for b in range(n // size):
        mrow = b * size + half - 1
        pieces.append(jnp.broadcast_to(p[mrow:mrow + 1, :], (size, w)))
    return jnp.concatenate(pieces, axis=0)


def _hgrn_kernel(q_ref, f_ref, i_ref, g_ref, lb_ref, og_ref, mask_ref, o_ref, st_ref):
    @pl.when(pl.program_id(1) == 0)
    def _():
        st_ref[...] = jnp.zeros_like(st_ref)

    row = lax.broadcasted_iota(jnp.int32, (CHUNK, 1), 0)
    lb = lb_ref[...]
    og = og_ref[...]

    def chunk_body(c, carry):
        rows = pl.ds(pl.multiple_of(c * CHUNK, CHUNK), CHUNK)
        q = q_ref[rows, :] * (HG_DK ** -0.5)
        f = lb + (1.0 - lb) * jax.nn.sigmoid(f_ref[rows, :])
        kk = 1.0 - f
        v = i_ref[rows, :]
        gate = g_ref[rows, :]
        p = jnp.log(f)
        decays = []
        for l in range(6):
            tb = _block_bcast(p, l, row)
            upper = ((row >> l) & 1) == 1
            decays.append(jnp.exp(jnp.where(upper, p, tb - p)))
            p = p + jnp.where(upper, tb, 0.0)
        last = p[CHUNK - 1:CHUNK, :]
        q_in = q * jnp.exp(p)
        k_out = kk * jnp.exp(last - p)
        st_scale = jnp.exp(last)

        for h in range(HG_HEADS):
            sl = slice(h * HG_DK, (h + 1) * HG_DK)
            qh, kh, vh = q[:, sl], kk[:, sl], v[:, sl].astype(BF16)
            scores = mask_ref[6] * _dot_nt(qh.astype(BF16), kh.astype(BF16))
            for l in range(6):
                dl = decays[l][:, sl]
                scores += mask_ref[l] * _dot_nt((qh * dl).astype(BF16), (kh * dl).astype(BF16))
            st = st_ref[h]
            o = _dot_nt(q_in[:, sl].astype(BF16), st.astype(BF16)) + _dot(scores.astype(BF16), vh)
            st_ref[h] = st * st_scale[:, sl] + _dot_tn(vh, k_out[:, sl].astype(BF16))
            o = _rms_rows(o, og) * (gate[:, sl] * jax.nn.sigmoid(gate[:, sl]))
            o_ref[rows, sl] = o.astype(o_ref.dtype)
        return carry

    lax.fori_loop(0, q_ref.shape[0] // CHUNK, chunk_body, 0)


def _hgrn(proj, lb, og, masks, batch, seq):
    m = proj.shape[0]
    ns = seq // HG_ROWS

    def col(c):
        return pl.BlockSpec((HG_ROWS, HG_W), lambda b, s: (b * ns + s, _HG_COL + c))

    return pl.pallas_call(
        _hgrn_kernel,
        grid=(batch, ns),
        in_specs=[col(0), col(1), col(2), col(3),
                  pl.BlockSpec((1, HG_W), lambda b, s: (0, 0)),
                  pl.BlockSpec((1, HG_DV), lambda b, s: (0, 0)),
                  pl.BlockSpec((7, CHUNK, CHUNK), lambda b, s: (0, 0, 0))],
        out_specs=pl.BlockSpec((HG_ROWS, HG_W), lambda b, s: (b * ns + s, 0)),
        out_shape=jax.ShapeDtypeStruct((m, HG_W), BF16),
        scratch_shapes=[pltpu.VMEM((HG_HEADS, HG_DV, HG_DK), F32)],
        compiler_params=_params("parallel", "arbitrary"),
        name="hgrn",
    )(proj, proj, proj, proj, lb, og, masks)


def _group_sumsq(x, ones_blk):
    x2 = x * x
    hi = x2.astype(BF16)
    lo = (x2 - hi.astype(F32)).astype(BF16)
    return _dot(hi, ones_blk) + _dot(lo, ones_blk)


def _df_prep_kernel(q_ref, k_ref, v_ref, qg_ref, kg_ref, ones_ref, qo_ref, ko_ref, vo_ref):
    ones_blk = ones_ref[...]
    q = q_ref[...]
    k = k_ref[...]
    qn = q * lax.rsqrt(_group_sumsq(q, ones_blk) * (1.0 / DF_DQK) + EPS) * qg_ref[...]
    kn = k * lax.rsqrt(_group_sumsq(k, ones_blk) * (1.0 / DF_DQK) + EPS) * kg_ref[...]
    qo_ref[...] = (qn * (DF_DQK ** -0.5)).astype(BF16)
    ko_ref[...] = kn.astype(BF16)
    vo_ref[...] = v_ref[...].astype(BF16)


def _df_prep(proj, qg, kg, ones_blk):
    m = proj.shape[0]

    def col(c):
        return pl.BlockSpec((ROW_TILE, DF_W), lambda i: (i, _DF_COL + c))

    row = pl.BlockSpec((ROW_TILE, DF_W), lambda i: (i, 0))
    vec = pl.BlockSpec((1, DF_W), lambda i: (0, 0))
    out = jax.ShapeDtypeStruct((m, DF_W), BF16)
    return pl.pallas_call(
        _df_prep_kernel,
        grid=(m // ROW_TILE,),
        in_specs=[col(0), col(1), col(2), vec, vec, pl.BlockSpec((DF_W, DF_W), lambda i: (0, 0))],
        out_specs=[row, row, row],
        out_shape=[out, out, out],
        compiler_params=_params("parallel"),
        name="df_prep",
    )(proj, proj, proj, qg, kg, ones_blk)


def _attn_kernel(*refs, n_maps, has_bias, finish):
    if has_bias:
        q_ref, k_ref, v_ref, b0_ref, b1_ref, far_ref, *rest = refs
    else:
        q_ref, k_ref, v_ref, b0_ref, *rest = refs
        b1_ref = far_ref = None
    *extra, o_ref, m_ref, l_ref, acc_ref = rest
    i = pl.program_id(2)
    t = q_ref.shape[0]
    q = q_ref[...]
    if n_maps == 2:
        lane = lax.broadcasted_iota(jnp.int32, q.shape, 1)
        zero = jnp.zeros_like(q)
        qs = [jnp.where(lane < DF_DQK, q, zero), jnp.where(lane >= DF_DQK, q, zero)]
    else:
        qs = [q]

    m_ref[...] = jnp.full_like(m_ref, NEG_BIG)
    l_ref[...] = jnp.zeros_like(l_ref)
    acc_ref[...] = jnp.zeros_like(acc_ref)

    def attend(start, bias):
        rows = pl.ds(pl.multiple_of(start, t), t)
        kt = k_ref[rows, :]
        vt = v_ref[rows, :]
        for mi in range(n_maps):
            s = _dot_nt(qs[mi], kt)
            if bias is not None:
                s = s + bias
            m_prev = m_ref[mi]
            m_new = jnp.maximum(m_prev, jnp.max(s, axis=-1, keepdims=True))
            alpha = jnp.exp(m_prev - m_new)
            p = jnp.exp(s - m_new)
            l_ref[mi] = alpha * l_ref[mi] + jnp.sum(p, axis=-1, keepdims=True)
            acc_ref[mi] = alpha * acc_ref[mi] + _dot(p.astype(BF16), vt)
            m_ref[mi] = m_new

    def far_body(j, carry):
        attend(j * t, far_ref[0, 0:1, 0:1] if has_bias else None)
        return carry

    lax.fori_loop(0, i - 1, far_body, 0)

    @pl.when(i >= 1)
    def _():
        attend((i - 1) * t, b1_ref[0] if has_bias else None)

    attend(i * t, b0_ref[0] if has_bias else b0_ref[...])

    outs = [acc_ref[mi] / l_ref[mi] for mi in range(n_maps)]
    o_ref[...] = finish(outs, *extra).astype(o_ref.dtype)


def _df_finish(outs, lam_ref, sg_ref, *, lambda_init):
    o = outs[0] - lam_ref[...] * outs[1]
    return _rms_rows(o, sg_ref[...]) * (1.0 - lambda_init)


def _ml_finish(outs):
    return outs[0]


def _df_attn(qn, kn, vb, b0, b1, far, lam, sg, batch, seq, lambda_init):
    m = qn.shape[0]
    t = ATT_TILE
    nq = seq // t
    kv = pl.BlockSpec((seq, DF_DV), lambda b, h, i: (b, h))
    tile = pl.BlockSpec((1, t, t), lambda b, h, i: (h, 0, 0))
    vec = pl.BlockSpec((1, DF_DV), lambda b, h, i: (0, 0))
    kern = functools.partial(_attn_kernel, n_maps=2, has_bias=True,
                             finish=functools.partial(_df_finish, lambda_init=lambda_init))
    return pl.pallas_call(
        kern,
        grid=(batch, DF_HEADS, nq),
        in_specs=[pl.BlockSpec((t, DF_DV), lambda b, h, i: (b * nq + i, h)), kv, kv, tile, tile,
                  pl.BlockSpec((1, 8, LANES), lambda b, h, i: (h, 0, 0)), vec, vec],
        out_specs=pl.BlockSpec((t, DF_DV), lambda b, h, i: (b * nq + i, h)),
        out_shape=jax.ShapeDtypeStruct((m, DF_W), BF16),
        scratch_shapes=[pltpu.VMEM((2, t, 1), F32), pltpu.VMEM((2, t, 1), F32),
                        pltpu.VMEM((2, t, DF_DV), F32)],
        compiler_params=_params("parallel", "parallel", "arbitrary"),
        name="df_attn",
    )(qn, kn, vb, b0, b1, far, lam, sg)


def _ml_attn(qh, kh, vb, mask, batch, seq):
    m = qh.shape[0]
    t = ATT_TILE
    nq = seq // t
    return pl.pallas_call(
        functools.partial(_attn_kernel, n_maps=1, has_bias=False, finish=_ml_finish),
        grid=(batch, ML_HEADS, nq),
        in_specs=[pl.BlockSpec((t, ML_HEAD_PAD), lambda b, h, i: (b * nq + i, h)),
                  pl.BlockSpec((seq, ML_HEAD_PAD), lambda b, h, i: (b, h)),
                  pl.BlockSpec((seq, ML_DV), lambda b, h, i: (b, h)),
                  pl.BlockSpec((t, t), lambda b, h, i: (0, 0))],
        out_specs=pl.BlockSpec((t, ML_DV), lambda b, h, i: (b * nq + i, h)),
        out_shape=jax.ShapeDtypeStruct((m, ML_W), BF16),
        scratch_shapes=[pltpu.VMEM((1, t, 1), F32), pltpu.VMEM((1, t, 1), F32),
                        pltpu.VMEM((1, t, ML_DV), F32)],
        compiler_params=_params("parallel", "parallel", "arbitrary"),
        name="ml_attn",
    )(qh, kh, vb, mask)


def _rope_pad(x, cos, sin, lane):
    rot = jnp.where(lane < ML_ROPE // 2, pltpu.roll(x, LANES - ML_ROPE // 2, 1),
                    pltpu.roll(x, ML_ROPE // 2, 1))
    return x * cos + rot * sin


def _ml_prep_kernel(cq_ref, ckv_ref, kr_ref, qln_ref, kvln_ref, wqn_ref, wqr_ref, wkn_ref, wv_ref,
                    qgn_ref, qgr_ref, kgn_ref, kgr_ref, cos_ref, sin_ref, qo_ref, ko_ref, vo_ref):
    cos = cos_ref[...]
    sin = sin_ref[...]
    lane = lax.broadcasted_iota(jnp.int32, cos.shape, 1)
    cq = _rms_rows(cq_ref[...], qln_ref[...]).astype(BF16)
    ckv = _rms_rows(ckv_ref[...], kvln_ref[...]).astype(BF16)
    q_nope = _dot(cq, wqn_ref[...])
    q_rope = _dot(cq, wqr_ref[...])
    k_nope = _dot(ckv, wkn_ref[...])
    vo_ref[...] = _dot(ckv, wv_ref[...]).astype(BF16)
    k_rope = kr_ref[...]
    kr_ss = jnp.sum(k_rope * k_rope, axis=-1, keepdims=True)
    kr_rot = _rope_pad(k_rope * kgr_ref[...], cos, sin, lane)
    qgn, qgr, kgn = qgn_ref[...], qgr_ref[...], kgn_ref[...]
    for h in range(ML_HEADS):
        sl = slice(h * LANES, (h + 1) * LANES)
        qn, qr, kn = q_nope[:, sl], q_rope[:, sl], k_nope[:, sl]
        q_ss = jnp.sum(qn * qn, axis=-1, keepdims=True) + jnp.sum(qr * qr, axis=-1, keepdims=True)
        q_rs = lax.rsqrt(q_ss * (1.0 / ML_DQK) + EPS) * (ML_DQK ** -0.5)
        k_ss = jnp.sum(kn * kn, axis=-1, keepdims=True) + kr_ss
        k_rs = lax.rsqrt(k_ss * (1.0 / ML_DQK) + EPS)
        base = h * ML_HEAD_PAD
        qo_ref[:, base:base + LANES] = (qn * q_rs * qgn).astype(BF16)
        qo_ref[:, base + LANES:base + 2 * LANES] = (_rope_pad(qr * qgr, cos, sin, lane) * q_rs).astype(BF16)
        ko_ref[:, base:base + LANES] = (kn * k_rs * kgn).astype(BF16)
        ko_ref[:, base + LANES:base + 2 * LANES] = (kr_rot * k_rs).astype(BF16)


def _ml_prep(proj, qln, kvln, wqn, wqr, wkn, wv, qgn, qgr, kgn, kgr, cos, sin, seq, layer):
    m = proj.shape[0]
    npos = seq // ROW_TILE

    def full(a):
        nd = a.ndim
        if nd == 3:
            return pl.BlockSpec((None,) + a.shape[1:], lambda i: (layer, 0, 0))
        return pl.BlockSpec(a.shape, lambda i: (0, 0))

    pos = pl.BlockSpec((ROW_TILE, LANES), lambda i: (i % npos, 0))
    hp = ML_HEADS * ML_HEAD_PAD
    return pl.pallas_call(
        _ml_prep_kernel,
        grid=(m // ROW_TILE,),
        in_specs=[pl.BlockSpec((ROW_TILE, ML_Q_RANK), lambda i: (i, _MLQ_COL)),
                  pl.BlockSpec((ROW_TILE, ML_KV_RANK), lambda i: (i, _MLKV_COL)),
                  pl.BlockSpec((ROW_TILE, LANES), lambda i: (i, _MLR_COL)),
                  full(qln), full(kvln), full(wqn), full(wqr), full(wkn), full(wv),
                  full(qgn), full(qgr), full(kgn), full(kgr), pos, pos],
        out_specs=[pl.BlockSpec((ROW_TILE, hp), lambda i: (i, 0)),
                   pl.BlockSpec((ROW_TILE, hp), lambda i: (i, 0)),
                   pl.BlockSpec((ROW_TILE, ML_W), lambda i: (i, 0))],
        out_shape=[jax.ShapeDtypeStruct((m, hp), BF16), jax.ShapeDtypeStruct((m, hp), BF16),
                   jax.ShapeDtypeStruct((m, ML_W), BF16)],
        compiler_params=_params("parallel"),
        name="ml_prep",
    )(proj, proj, proj, qln, kvln, wqn, wqr, wkn, wv, qgn, qgr, kgn, kgr, cos, sin)


def _out_kernel(x_ref, hg_ref, df_ref, ml_ref, w1_ref, w2_ref, w3_ref, o_ref):
    o_ref[...] = (x_ref[...] + _dot(hg_ref[...], w1_ref[...]) + _dot(df_ref[...], w2_ref[...])
                  + _dot(ml_ref[...], w3_ref[...]))


def _out_proj(x, o_hg, o_df, o_ml, w1, w2, w3, layer):
    m, d = x.shape

    def rows(w):
        return pl.BlockSpec((ROW_TILE, w), lambda i: (i, 0))

    def wspec(w):
        return pl.BlockSpec((None,) + w.shape[1:], lambda i: (layer, 0, 0))

    return pl.pallas_call(
        _out_kernel,
        grid=(m // ROW_TILE,),
        in_specs=[rows(d), rows(HG_W), rows(DF_W), rows(ML_W), wspec(w1), wspec(w2), wspec(w3)],
        out_specs=rows(d),
        out_shape=jax.ShapeDtypeStruct((m, d), F32),
        compiler_params=_params("parallel"),
        name="out_proj",
    )(x, o_hg, o_df, o_ml, w1, w2, w3)


def _t5_bucket(rel):
    half = REL_BUCKETS // 2
    max_exact = half // 2
    ret = (rel > 0).astype(jnp.int32) * half
    n = jnp.abs(rel)
    large = max_exact + (jnp.log(jnp.maximum(n, 1).astype(F32) / max_exact)
                         / math.log(REL_MAX_DIST / max_exact) * (half - max_exact)).astype(jnp.int32)
    large = jnp.minimum(large, half - 1)
    return ret + jnp.where(n < max_exact, n, large)


def _chunk_mask_tile(t):
    ii = jnp.arange(t)
    return (ii[None, :] // CHUNK) <= (ii[:, None] // CHUNK)


def _bias_tiles(rel_bias, t):
    assert t >= REL_MAX_DIST and t % CHUNK == 0
    table = rel_bias.astype(F32)
    ii = jnp.arange(t)
    rel = ii[None, :] - ii[:, None]
    b0 = table[_t5_bucket(rel)].transpose(2, 0, 1)
    b0 = jnp.where(_chunk_mask_tile(t)[None], b0, NEG_BIG)
    b1 = table[_t5_bucket(rel - t)].transpose(2, 0, 1)
    far = table[_t5_bucket(jnp.full((), -2 * t, jnp.int32))]
    far = jnp.broadcast_to(far[:, None, None], (DF_HEADS, 8, LANES))
    return b0, b1, far


def _rope_tables(seq):
    r = ML_ROPE
    freqs = ROPE_BASE ** (-jnp.arange(0, r, 2, dtype=F32) / r)
    ang = jnp.arange(seq).astype(F32)[:, None] * freqs[None, :]
    cos, sin = jnp.cos(ang), jnp.sin(ang)
    z = jnp.zeros((seq, LANES - r), F32)
    return jnp.concatenate([cos, cos, z], axis=-1), jnp.concatenate([-sin, sin, z], axis=-1)


def _pad_last(a, n):
    return jnp.pad(a, [(0, 0)] * (a.ndim - 1) + [(0, n - a.shape[-1])])


def _pad_heads(a, width):
    lead = a.shape[:-1]
    a = a.reshape(lead + (ML_HEADS, width))
    return _pad_last(a, LANES).reshape(lead + (ML_HEADS * LANES,))


def kernel(x, ffn_a_norm, ffn_a_w_gate, ffn_a_w_up, ffn_a_w_down, mix_norm, w_in, w_out, hgrn_lb_logits, hgrn_out_norm, diff_q_norm, diff_k_norm, diff_lambda_q1, diff_lambda_k1, diff_lambda_q2, diff_lambda_k2, diff_subln, rel_bias, mla_q_lora_norm, mla_w_uq, mla_kv_lora_norm, mla_w_ukv, mla_q_norm, mla_k_norm, ffn_b_norm, ffn_b_w_gate, ffn_b_w_up, ffn_b_w_down):
    batch, seq, d = x.shape
    depth = w_in.shape[0]
    m = batch * seq
    assert m % ROW_TILE == 0 and seq % ROW_TILE == 0 and seq % ATT_TILE == 0 and seq % HG_ROWS == 0

    def ffn_weights(wg, wu, wd):
        wg = _pad_last(wg, D_FF_PAD).astype(BF16)
        wu = _pad_last(wu, D_FF_PAD).astype(BF16)
        wd = jnp.pad(wd, ((0, 0), (0, D_FF_PAD - D_FF), (0, 0))).astype(BF16)
        return wg, wu, wd

    wa = ffn_weights(ffn_a_w_gate, ffn_a_w_up, ffn_a_w_down)
    wb = ffn_weights(ffn_b_w_gate, ffn_b_w_up, ffn_b_w_down)
    w_in_p = _pad_last(w_in, P_IN_PAD).astype(BF16)
    w_o = w_out.astype(BF16)
    w_o1, w_o2, w_o3 = w_o[:, :HG_W], w_o[:, HG_W:HG_W + DF_W], w_o[:, HG_W + DF_W:]

    uq = mla_w_uq.reshape(depth, ML_Q_RANK, ML_HEADS, ML_DQK)
    wqn = uq[..., :ML_NOPE].reshape(depth, ML_Q_RANK, ML_HEADS * ML_NOPE).astype(BF16)
    wqr = _pad_heads(uq[..., ML_NOPE:].reshape(depth, ML_Q_RANK, ML_HEADS * ML_ROPE), ML_ROPE).astype(BF16)
    ukv = mla_w_ukv.reshape(depth, ML_KV_RANK, ML_HEADS, ML_NOPE + ML_DV)
    wkn = ukv[..., :ML_NOPE].reshape(depth, ML_KV_RANK, ML_HEADS * ML_NOPE).astype(BF16)
    wv = ukv[..., ML_NOPE:].reshape(depth, ML_KV_RANK, ML_HEADS * ML_DV).astype(BF16)

    lb_all = jnp.cumsum(jax.nn.softmax(hgrn_lb_logits.astype(F32), axis=0), axis=0)
    lb_all = lb_all - lb_all[0:1]

    hg_masks = jnp.asarray(_hgrn_level_masks())
    b0, b1, far = _bias_tiles(rel_bias, ATT_TILE)
    ml_mask = jnp.where(_chunk_mask_tile(ATT_TILE), 0.0, NEG_BIG).astype(F32)
    cos, sin = _rope_tables(seq)
    ones_blk = jnp.asarray(np.kron(np.eye(DF_W // DF_DQK), np.ones((DF_DQK, DF_DQK))), BF16)

    def row3(a):
        return a.reshape(a.shape[0], 1, a.shape[1])

    g_a, g_mix, g_b = row3(ffn_a_norm), row3(mix_norm), row3(ffn_b_norm)
    g_qln, g_kvln = row3(mla_q_lora_norm), row3(mla_kv_lora_norm)

    xf = x.reshape(m, d)
    for l in range(depth):
        xf = _ffn(xf, g_a, *wa, l)

        proj = _proj(xf, g_mix, w_in_p, l)
        o_hg = _hgrn(proj, lb_all[l][None, :], hgrn_out_norm[l][None, :], hg_masks, batch, seq)

        lambda_init = 0.8 - 0.6 * math.exp(-0.3 * l)
        lam = (jnp.exp(jnp.sum(diff_lambda_q1[l].astype(F32) * diff_lambda_k1[l].astype(F32)))
               - jnp.exp(jnp.sum(diff_lambda_q2[l].astype(F32) * diff_lambda_k2[l].astype(F32)))
               + lambda_init)
        qn, kn, vb = _df_prep(proj, jnp.tile(diff_q_norm[l], DF_W // DF_DQK)[None, :],
                              jnp.tile(diff_k_norm[l], DF_W // DF_DQK)[None, :], ones_blk)
        o_df = _df_attn(qn, kn, vb, b0, b1, far, jnp.full((1, DF_DV), lam, F32),
                        diff_subln[l][None, :], batch, seq, lambda_init)

        qg, kg = mla_q_norm[l], mla_k_norm[l]
        qh, kh, vm = _ml_prep(proj, g_qln, g_kvln, wqn, wqr, wkn, wv,
                              qg[None, :ML_NOPE], _pad_last(qg[None, ML_NOPE:], LANES),
                              kg[None, :ML_NOPE], _pad_last(kg[None, ML_NOPE:], LANES),
                              cos, sin, seq, l)
        o_ml = _ml_attn(qh, kh, vm, ml_mask, batch, seq)

        xf = _out_proj(xf, o_hg, o_df, o_ml, w_o1, w_o2, w_o3, l)
        xf = _ffn(xf, g_b, *wb, l)
    return xf.reshape(batch, seq, d)
```

```python
import functools
import math

import jax
import jax.numpy as jnp
import numpy as np
from jax import lax
from jax.experimental import pallas as pl
from jax.experimental.pallas import tpu as pltpu

F32 = jnp.float32
BF16 = jnp.bfloat16

D_MODEL = 2048
DEPTH = 4
CHUNK = 64
D_FF = 5504
EPS = 1e-6
HG_HEADS = 6
HG_DK = 128
HG_DV = 128
HG_W = HG_HEADS * HG_DK
DF_HEADS = 4
DF_DQK = 64
DF_DV = 128
DF_W = DF_HEADS * DF_DV
ML_HEADS = 6
ML_Q_RANK = 512
ML_KV_RANK = 256
ML_NOPE = 128
ML_ROPE = 64
ML_DV = 128
ML_DQK = ML_NOPE + ML_ROPE
ML_W = ML_HEADS * ML_DV
ROPE_BASE = 10000.0
REL_BUCKETS = 32
REL_MAX_DIST = 128
P_IN = 4 * HG_W + 3 * DF_W + ML_Q_RANK + ML_KV_RANK + ML_ROPE

LANES = 128
ML_HEAD_PAD = 2 * LANES
FF_TILE = 512
D_FF_PAD = -(-D_FF // FF_TILE) * FF_TILE
P_IN_PAD = -(-P_IN // FF_TILE) * FF_TILE
ROW_TILE = 512
HG_ROWS = 256
ATT_TILE = 512
ATT_ROWS = 256
LOG2E = math.log2(math.e)
NEG_BIG = -1e30
VMEM_LIMIT = 56 * 1024 * 1024

_HG_COL = 0
_DF_COL = (4 * HG_W) // DF_W
_MLQ_COL = (4 * HG_W + 3 * DF_W) // ML_Q_RANK
_MLKV_COL = (4 * HG_W + 3 * DF_W + ML_Q_RANK) // ML_KV_RANK
_MLR_COL = (4 * HG_W + 3 * DF_W + ML_Q_RANK + ML_KV_RANK) // LANES


def _dot(a, b):
    return jnp.dot(a, b, preferred_element_type=F32)


def _dot_nt(a, b):
    return lax.dot_general(a, b, (((1,), (1,)), ((), ())), preferred_element_type=F32)


def _dot_tn(a, b):
    return lax.dot_general(a, b, (((0,), (0,)), ((), ())), preferred_element_type=F32)


def _params(*sem):
    return pltpu.CompilerParams(dimension_semantics=sem, vmem_limit_bytes=VMEM_LIMIT)


def _rms_rows(x, gain):
    return x * lax.rsqrt(jnp.mean(x * x, axis=-1, keepdims=True) + EPS) * gain


def _ffn_kernel(x_ref, g_ref, wg_ref, wu_ref, wd_ref, o_ref, h_ref):
    @pl.when(pl.program_id(1) == 0)
    def _():
        x = x_ref[...]
        h_ref[...] = _rms_rows(x, g_ref[...]).astype(BF16)
        o_ref[...] = x

    h = h_ref[...]
    a = _dot(h, wg_ref[...])
    u = _dot(h, wu_ref[...])
    act = (0.5 * a) * jax.nn.sigmoid(a) * u
    o_ref[...] += _dot(act.astype(BF16), wd_ref[...])


def _ffn(x, gains, wg, wu, wd, layer):
    m, d = x.shape
    fp = wg.shape[-1]
    return pl.pallas_call(
        _ffn_kernel,
        grid=(m // ROW_TILE, fp // FF_TILE),
        in_specs=[
            pl.BlockSpec((ROW_TILE, d), lambda i, j: (i, 0)),
            pl.BlockSpec((None, 1, d), lambda i, j: (layer, 0, 0)),
            pl.BlockSpec((None, d, FF_TILE), lambda i, j: (layer, 0, j)),
            pl.BlockSpec((None, d, FF_TILE), lambda i, j: (layer, 0, j)),
            pl.BlockSpec((None, FF_TILE, d), lambda i, j: (layer, j, 0)),
        ],
        out_specs=pl.BlockSpec((ROW_TILE, d), lambda i, j: (i, 0)),
        out_shape=jax.ShapeDtypeStruct((m, d), F32),
        scratch_shapes=[pltpu.VMEM((ROW_TILE, d), BF16)],
        compiler_params=_params("parallel", "arbitrary"),
        name="ffn",
    )(x, gains, wg, wu, wd)


def _proj_kernel(x_ref, g_ref, w_ref, o_ref, h_ref):
    @pl.when(pl.program_id(1) == 0)
    def _():
        h_ref[...] = _rms_rows(x_ref[...], g_ref[...]).astype(BF16)

    o_ref[...] = _dot(h_ref[...], w_ref[...])


def _proj(x, gains, w, layer):
    m, d = x.shape
    n = w.shape[-1]
    return pl.pallas_call(
        _proj_kernel,
        grid=(m // ROW_TILE, n // FF_TILE),
        in_specs=[
            pl.BlockSpec((ROW_TILE, d), lambda i, j: (i, 0)),
            pl.BlockSpec((None, 1, d), lambda i, j: (layer, 0, 0)),
            pl.BlockSpec((None, d, FF_TILE), lambda i, j: (layer, 0, j)),
        ],
        out_specs=pl.BlockSpec((ROW_TILE, FF_TILE), lambda i, j: (i, j)),
        out_shape=jax.ShapeDtypeStruct((m, n), F32),
        scratch_shapes=[pltpu.VMEM((ROW_TILE, d), BF16)],
        compiler_params=_params("parallel", "arbitrary"),
        name="proj",
    )(x, gains, w)


def _hgrn_level_masks():
    t = np.arange(CHUNK)[:, None]
    s = np.arange(CHUNK)[None, :]
    levels = [(((t >> l) ^ (s >> l)) == 1) & (t > s) for l in range(6)]
    levels.append(t == s)
    return np.stack(levels).astype(np.float32)


def _block_bcast(p, level, row):
    n, w = p.shape
    half = 1 << level
    size = 2 * half
    if level == 0:
        return jnp.where((row & 1) == 0, p, pltpu.roll(p, 1, 0))
    if level == 1:
        r = row & 3
        return jnp.where(r == 0, pltpu.roll(p, n - 1, 0),
                         jnp.where(r == 1, p,
                                   jnp.where(r == 2, pltpu.roll(p, 1, 0), pltpu.roll(p, 2, 0))))
    pieces = []
    for b in range(n // size):
        mrow = b * size + half - 1
        pieces.append(jnp.broadcast_to(p[mrow:mrow + 1, :], (size, w)))
    return jnp.concatenate(pieces, axis=0)


def _hgrn_kernel(q_ref, f_ref, i_ref, g_ref, lb_ref, og_ref, mask_ref, o_ref, st_ref):
    @pl.when(pl.program_id(1) == 0)
    def _():
        st_ref[...] = jnp.zeros_like(st_ref)

    row = lax.broadcasted_iota(jnp.int32, (CHUNK, 1), 0)
    lb = lb_ref[...]
    og = og_ref[...]

    def chunk_body(c, carry):
        rows = pl.ds(pl.multiple_of(c * CHUNK, CHUNK), CHUNK)
        q = q_ref[rows, :] * (HG_DK ** -0.5)
        f = lb + (1.0 - lb) * jax.nn.sigmoid(f_ref[rows, :])
        kk = 1.0 - f
        v = i_ref[rows, :]
        gate = g_ref[rows, :]
        p = jnp.log(f)
        decays = []
        for l in range(6):
            tb = _block_bcast(p, l, row)
            upper = ((row >> l) & 1) == 1
            decays.append(jnp.exp(jnp.where(upper, p, tb - p)))
            p = p + jnp.where(upper, tb, 0.0)
        last = p[CHUNK - 1:CHUNK, :]
        q_in = q * jnp.exp(p)
        k_out = kk * jnp.exp(last - p)
        st_scale = jnp.exp(last)

        for h in range(HG_HEADS):
            sl = slice(h * HG_DK, (h + 1) * HG_DK)
            qh, kh, vh = q[:, sl], kk[:, sl], v[:, sl].astype(BF16)
            scores = mask_ref[6] * _dot_nt(qh.astype(BF16), kh.astype(BF16))
            for l in range(6):
                dl = decays[l][:, sl]
                scores += mask_ref[l] * _dot_nt((qh * dl).astype(BF16), (kh * dl).astype(BF16))
            st = st_ref[h]
            o = _dot_nt(q_in[:, sl].astype(BF16), st.astype(BF16)) + _dot(scores.astype(BF16), vh)
            st_ref[h] = st * st_scale[:, sl] + _dot_tn(vh, k_out[:, sl].astype(BF16))
            o = _rms_rows(o, og) * (gate[:, sl] * jax.nn.sigmoid(gate[:, sl]))
            o_ref[rows, sl] = o.astype(o_ref.dtype)
        return carry

    lax.fori_loop(0, q_ref.shape[0] // CHUNK, chunk_body, 0)


def _hgrn(proj, lb, og, masks, batch, seq):
    m = proj.shape[0]
    ns = seq // HG_ROWS

    def col(c):
        return pl.BlockSpec((HG_ROWS, HG_W), lambda b, s: (b * ns + s, _HG_COL + c))

    return pl.pallas_call(
        _hgrn_kernel,
        grid=(batch, ns),
        in_specs=[col(0), col(1), col(2), col(3),
                  pl.BlockSpec((1, HG_W), lambda b, s: (0, 0)),
                  pl.BlockSpec((1, HG_DV), lambda b, s: (0, 0)),
                  pl.BlockSpec((7, CHUNK, CHUNK), lambda b, s: (0, 0, 0))],
        out_specs=pl.BlockSpec((HG_ROWS, HG_W), lambda b, s: (b * ns + s, 0)),
        out_shape=jax.ShapeDtypeStruct((m, HG_W), BF16),
        scratch_shapes=[pltpu.VMEM((HG_HEADS, HG_DV, HG_DK), F32)],
        compiler_params=_params("parallel", "arbitrary"),
        name="hgrn",
    )(proj, proj, proj, proj, lb, og, masks)


def _group_sumsq(x, ones_blk):
    x2 = x * x
    hi = x2.astype(BF16)
    lo = (x2 - hi.astype(F32)).astype(BF16)
    return _dot(hi, ones_blk) + _dot(lo, ones_blk)


def _df_prep_kernel(q_ref, k_ref, v_ref, qg_ref, kg_ref, ones_ref, qo_ref, ko_ref, vo_ref):
    ones_blk = ones_ref[...]
    q = q_ref[...]
    k = k_ref[...]
    qn = q * lax.rsqrt(_group_sumsq(q, ones_blk) * (1.0 / DF_DQK) + EPS) * qg_ref[...]
    kn = k * lax.rsqrt(_group_sumsq(k, ones_blk) * (1.0 / DF_DQK) + EPS) * kg_ref[...]
    qo_ref[...] = (qn * (DF_DQK ** -0.5 * LOG2E)).astype(BF16)
    ko_ref[...] = kn.astype(BF16)
    vo_ref[...] = v_ref[...].astype(BF16)


def _df_prep(proj, qg, kg, ones_blk):
    m = proj.shape[0]

    def col(c):
        return pl.BlockSpec((ROW_TILE, DF_W), lambda i: (i, _DF_COL + c))

    row = pl.BlockSpec((ROW_TILE, DF_W), lambda i: (i, 0))
    vec = pl.BlockSpec((1, DF_W), lambda i: (0, 0))
    out = jax.ShapeDtypeStruct((m, DF_W), BF16)
    return pl.pallas_call(
        _df_prep_kernel,
        grid=(m // ROW_TILE,),
        in_specs=[col(0), col(1), col(2), vec, vec, pl.BlockSpec((DF_W, DF_W), lambda i: (0, 0))],
        out_specs=[row, row, row],
        out_shape=[out, out, out],
        compiler_params=_params("parallel"),
        name="df_prep",
    )(proj, proj, proj, qg, kg, ones_blk)


def _attn_kernel(*refs, n_maps, has_bias, finish):
    if has_bias:
        q_ref, k_ref, v_ref, b0_ref, b1_ref, far_ref, *rest = refs
    else:
        q_ref, k_ref, v_ref, b0_ref, *rest = refs
        b1_ref = far_ref = None
    *extra, o_ref, m_ref, l_ref, acc_ref = rest
    i = pl.program_id(2)
    t = q_ref.shape[0]
    nc = t // LANES
    if n_maps == 2:
        q = q_ref[...]
        lane = lax.broadcasted_iota(jnp.int32, q.shape, 1)
        zero = jnp.zeros_like(q)
        qs = [jnp.where(lane < DF_DQK, q, zero), jnp.where(lane >= DF_DQK, q, zero)]
    else:
        qs = [q_ref[...]]

    m_ref[...] = jnp.full_like(m_ref, NEG_BIG)
    l_ref[...] = jnp.zeros_like(l_ref)
    acc_ref[...] = jnp.zeros_like(acc_ref)

    def tile_cols(ref, r):
        return [ref[0, r, c * LANES:(c + 1) * LANES] for c in range(nc)]

    def attend(start, n_cols, biases, const):
        rows = pl.ds(pl.multiple_of(start, t), LANES * n_cols)
        kt = k_ref[rows, :]
        vt = v_ref[rows, :]
        for mi in range(n_maps):
            for r0 in range(0, t, ATT_ROWS):
                r = slice(r0, r0 + ATT_ROWS)
                s = _dot_nt(qs[mi][r], kt)
                cols = [s[:, c * LANES:(c + 1) * LANES] for c in range(n_cols)]
                if biases is not None:
                    cols = [x if b is None else x + b for x, b in zip(cols, biases(r))]
                m_prev = m_ref[mi, r]
                row_max = jnp.max(functools.reduce(jnp.maximum, cols), axis=-1, keepdims=True)
                if const is None:
                    m_new = jnp.maximum(m_prev, row_max)
                    shift = m_new
                else:
                    m_new = jnp.maximum(m_prev, row_max + const)
                    shift = m_new - const
                alpha = jnp.exp2(m_prev - m_new)
                ps = [jnp.exp2(x - shift) for x in cols]
                l_ref[mi, r] = alpha * l_ref[mi, r] + functools.reduce(jnp.add, ps)
                p = jnp.concatenate([x.astype(BF16) for x in ps], axis=-1)
                acc_ref[mi, r] = alpha * acc_ref[mi, r] + _dot(p, vt)
                m_ref[mi, r] = m_new

    def far_body(j, carry):
        attend(j * t, nc, None, far_ref[0, 0:1, 0:1] if has_bias else None)
        return carry

    lax.fori_loop(0, i - 1, far_body, 0)

    @pl.when(i >= 1)
    def _():
        prev = (lambda r: tile_cols(b1_ref, r)) if has_bias else (lambda r: [None] * nc)
        attend((i - 1) * t, 2 * nc, lambda r: prev(r) + tile_cols(b0_ref, r), None)

    @pl.when(i == 0)
    def _():
        attend(0, nc, lambda r: tile_cols(b0_ref, r), None)

    outs = [acc_ref[mi] / jnp.sum(l_ref[mi], axis=-1, keepdims=True) for mi in range(n_maps)]
    o_ref[...] = finish(outs, *extra).astype(o_ref.dtype)


def _df_finish(outs, lam_ref, sg_ref, *, lambda_init):
    o = outs[0] - lam_ref[...] * outs[1]
    return _rms_rows(o, sg_ref[...]) * (1.0 - lambda_init)


def _ml_finish(outs):
    return outs[0]


def _df_attn(qn, kn, vb, b0, b1, far, lam, sg, batch, seq, lambda_init):
    m = qn.shape[0]
    t = ATT_TILE
    nq = seq // t
    kv = pl.BlockSpec((seq, DF_DV), lambda b, h, i: (b, h))
    tile = pl.BlockSpec((1, t, t), lambda b, h, i: (h, 0, 0))
    vec = pl.BlockSpec((1, DF_DV), lambda b, h, i: (0, 0))
    kern = functools.partial(_attn_kernel, n_maps=2, has_bias=True,
                             finish=functools.partial(_df_finish, lambda_init=lambda_init))
    return pl.pallas_call(
        kern,
        grid=(batch, DF_HEADS, nq),
        in_specs=[pl.BlockSpec((t, DF_DV), lambda b, h, i: (b * nq + i, h)), kv, kv, tile, tile,
                  pl.BlockSpec((1, 8, LANES), lambda b, h, i: (h, 0, 0)), vec, vec],
        out_specs=pl.BlockSpec((t, DF_DV), lambda b, h, i: (b * nq + i, h)),
        out_shape=jax.ShapeDtypeStruct((m, DF_W), BF16),
        scratch_shapes=[pltpu.VMEM((2, t, LANES), F32), pltpu.VMEM((2, t, LANES), F32),
                        pltpu.VMEM((2, t, DF_DV), F32)],
        compiler_params=_params("parallel", "parallel", "arbitrary"),
        name="df_attn",
    )(qn, kn, vb, b0, b1, far, lam, sg)


def _ml_attn(qh, kh, vb, mask, batch, seq):
    m = qh.shape[0]
    t = ATT_TILE
    nq = seq // t
    return pl.pallas_call(
        functools.partial(_attn_kernel, n_maps=1, has_bias=False, finish=_ml_finish),
        grid=(batch, ML_HEADS, nq),
        in_specs=[pl.BlockSpec((t, ML_HEAD_PAD), lambda b, h, i: (b * nq + i, h)),
                  pl.BlockSpec((seq, ML_HEAD_PAD), lambda b, h, i: (b, h)),
                  pl.BlockSpec((seq, ML_DV), lambda b, h, i: (b, h)),
                  pl.BlockSpec((1, t, t), lambda b, h, i: (0, 0, 0))],
        out_specs=pl.BlockSpec((t, ML_DV), lambda b, h, i: (b * nq + i, h)),
        out_shape=jax.ShapeDtypeStruct((m, ML_W), BF16),
        scratch_shapes=[pltpu.VMEM((1, t, LANES), F32), pltpu.VMEM((1, t, LANES), F32),
                        pltpu.VMEM((1, t, ML_DV), F32)],
        compiler_params=_params("parallel", "parallel", "arbitrary"),
        name="ml_attn",
    )(qh, kh, vb, mask)


def _rope_pad(x, cos, sin, lane):
    rot = jnp.where(lane < ML_ROPE // 2, pltpu.roll(x, LANES - ML_ROPE // 2, 1),
                    pltpu.roll(x, ML_ROPE // 2, 1))
    return x * cos + rot * sin


def _ml_prep_kernel(cq_ref, ckv_ref, kr_ref, qln_ref, kvln_ref, wqn_ref, wqr_ref, wkn_ref, wv_ref,
                    qgn_ref, qgr_ref, kgn_ref, kgr_ref, cos_ref, sin_ref, qo_ref, ko_ref, vo_ref):
    cos = cos_ref[...]
    sin = sin_ref[...]
    lane = lax.broadcasted_iota(jnp.int32, cos.shape, 1)
    cq = _rms_rows(cq_ref[...], qln_ref[...]).astype(BF16)
    ckv = _rms_rows(ckv_ref[...], kvln_ref[...]).astype(BF16)
    q_nope = _dot(cq, wqn_ref[...])
    q_rope = _dot(cq, wqr_ref[...])
    k_nope = _dot(ckv, wkn_ref[...])
    vo_ref[...] = _dot(ckv, wv_ref[...]).astype(BF16)
    k_rope = kr_ref[...]
    kr_ss = jnp.sum(k_rope * k_rope, axis=-1, keepdims=True)
    kr_rot = _rope_pad(k_rope * kgr_ref[...], cos, sin, lane)
    qgn, qgr, kgn = qgn_ref[...], qgr_ref[...], kgn_ref[...]
    for h in range(ML_HEADS):
        sl = slice(h * LANES, (h + 1) * LANES)
        qn, qr, kn = q_nope[:, sl], q_rope[:, sl], k_nope[:, sl]
        q_ss = jnp.sum(qn * qn, axis=-1, keepdims=True) + jnp.sum(qr * qr, axis=-1, keepdims=True)
        q_rs = lax.rsqrt(q_ss * (1.0 / ML_DQK) + EPS) * (ML_DQK ** -0.5 * LOG2E)
        k_ss = jnp.sum(kn * kn, axis=-1, keepdims=True) + kr_ss
        k_rs = lax.rsqrt(k_ss * (1.0 / ML_DQK) + EPS)
        base = h * ML_HEAD_PAD
        qo_ref[:, base:base + LANES] = (qn * q_rs * qgn).astype(BF16)
        qo_ref[:, base + LANES:base + 2 * LANES] = (_rope_pad(qr * qgr, cos, sin, lane) * q_rs).astype(BF16)
        ko_ref[:, base:base + LANES] = (kn * k_rs * kgn).astype(BF16)
        ko_ref[:, base + LANES:base + 2 * LANES] = (kr_rot * k_rs).astype(BF16)


def _ml_prep(proj, qln, kvln, wqn, wqr, wkn, wv, qgn, qgr, kgn, kgr, cos, sin, seq, layer):
    m = proj.shape[0]
    npos = seq // ROW_TILE

    def full(a):
        nd = a.ndim
        if nd == 3:
            return pl.BlockSpec((None,) + a.shape[1:], lambda i: (layer, 0, 0))
        return pl.BlockSpec(a.shape, lambda i: (0, 0))

    pos = pl.BlockSpec((ROW_TILE, LANES), lambda i: (i % npos, 0))
    hp = ML_HEADS * ML_HEAD_PAD
    return pl.pallas_call(
        _ml_prep_kernel,
        grid=(m // ROW_TILE,),
        in_specs=[pl.BlockSpec((ROW_TILE, ML_Q_RANK), lambda i: (i, _MLQ_COL)),
                  pl.BlockSpec((ROW_TILE, ML_KV_RANK), lambda i: (i, _MLKV_COL)),
                  pl.BlockSpec((ROW_TILE, LANES), lambda i: (i, _MLR_COL)),
                  full(qln), full(kvln), full(wqn), full(wqr), full(wkn), full(wv),
                  full(qgn), full(qgr), full(kgn), full(kgr), pos, pos],
        out_specs=[pl.BlockSpec((ROW_TILE, hp), lambda i: (i, 0)),
                   pl.BlockSpec((ROW_TILE, hp), lambda i: (i, 0)),
                   pl.BlockSpec((ROW_TILE, ML_W), lambda i: (i, 0))],
        out_shape=[jax.ShapeDtypeStruct((m, hp), BF16), jax.ShapeDtypeStruct((m, hp), BF16),
                   jax.ShapeDtypeStruct((m, ML_W), BF16)],
        compiler_params=_params("parallel"),
        name="ml_prep",
    )(proj, proj, proj, qln, kvln, wqn, wqr, wkn, wv, qgn, qgr, kgn, kgr, cos, sin)


def _out_kernel(x_ref, hg_ref, df_ref, ml_ref, w1_ref, w2_ref, w3_ref, o_ref):
    o_ref[...] = (x_ref[...] + _dot(hg_ref[...], w1_ref[...]) + _dot(df_ref[...], w2_ref[...])
                  + _dot(ml_ref[...], w3_ref[...]))


def _out_proj(x, o_hg, o_df, o_ml, w1, w2, w3, layer):
    m, d = x.shape

    def rows(w):
        return pl.BlockSpec((ROW_TILE, w), lambda i: (i, 0))

    def wspec(w):
        return pl.BlockSpec((None,) + w.shape[1:], lambda i: (layer, 0, 0))

    return pl.pallas_call(
        _out_kernel,
        grid=(m // ROW_TILE,),
        in_specs=[rows(d), rows(HG_W), rows(DF_W), rows(ML_W), wspec(w1), wspec(w2), wspec(w3)],
        out_specs=rows(d),
        out_shape=jax.ShapeDtypeStruct((m, d), F32),
        compiler_params=_params("parallel"),
        name="out_proj",
    )(x, o_hg, o_df, o_ml, w1, w2, w3)


def _t5_bucket(rel):
    half = REL_BUCKETS // 2
    max_exact = half // 2
    ret = (rel > 0).astype(jnp.int32) * half
    n = jnp.abs(rel)
    large = max_exact + (jnp.log(jnp.maximum(n, 1).astype(F32) / max_exact)
                         / math.log(REL_MAX_DIST / max_exact) * (half - max_exact)).astype(jnp.int32)
    large = jnp.minimum(large, half - 1)
    return ret + jnp.where(n < max_exact, n, large)


def _chunk_mask_tile(t):
    ii = jnp.arange(t)
    return (ii[None, :] // CHUNK) <= (ii[:, None] // CHUNK)


def _bias_tiles(rel_bias, t):
    assert t >= REL_MAX_DIST and t % CHUNK == 0
    table = rel_bias.astype(F32)

    def lookup(bucket):
        out = jnp.zeros((DF_HEADS,) + bucket.shape, F32)
        for b in range(REL_BUCKETS):
            out = out + jnp.where(bucket[None] == b, table[b][:, None, None], 0.0)
        return out

    ii = jnp.arange(t)
    rel = ii[None, :] - ii[:, None]
    b0 = jnp.where(_chunk_mask_tile(t)[None], lookup(_t5_bucket(rel)) * LOG2E, NEG_BIG)
    b1 = lookup(_t5_bucket(rel - t)) * LOG2E
    far = table[_t5_bucket(jnp.full((), -2 * t, jnp.int32))] * LOG2E
    far = jnp.broadcast_to(far[:, None, None], (DF_HEADS, 8, LANES))
    return b0, b1, far


def _rope_tables(seq):
    r = ML_ROPE
    freqs = ROPE_BASE ** (-jnp.arange(0, r, 2, dtype=F32) / r)
    ang = jnp.arange(seq).astype(F32)[:, None] * freqs[None, :]
    cos, sin = jnp.cos(ang), jnp.sin(ang)
    z = jnp.zeros((seq, LANES - r), F32)
    return jnp.concatenate([cos, cos, z], axis=-1), jnp.concatenate([-sin, sin, z], axis=-1)


def _pad_last(a, n):
    return jnp.pad(a, [(0, 0)] * (a.ndim - 1) + [(0, n - a.shape[-1])])


def _pad_heads(a, width):
    lead = a.shape[:-1]
    a = a.reshape(lead + (ML_HEADS, width))
    return _pad_last(a, LANES).reshape(lead + (ML_HEADS * LANES,))


def kernel(x, ffn_a_norm, ffn_a_w_gate, ffn_a_w_up, ffn_a_w_down, mix_norm, w_in, w_out, hgrn_lb_logits, hgrn_out_norm, diff_q_norm, diff_k_norm, diff_lambda_q1, diff_lambda_k1, diff_lambda_q2, diff_lambda_k2, diff_subln, rel_bias, mla_q_lora_norm, mla_w_uq, mla_kv_lora_norm, mla_w_ukv, mla_q_norm, mla_k_norm, ffn_b_norm, ffn_b_w_gate, ffn_b_w_up, ffn_b_w_down):
    batch, seq, d = x.shape
    depth = w_in.shape[0]
    m = batch * seq
    assert m % ROW_TILE == 0 and seq % ROW_TILE == 0 and seq % ATT_TILE == 0 and seq % HG_ROWS == 0

    def ffn_weights(wg, wu, wd):
        wg = _pad_last(wg.astype(BF16), D_FF_PAD)
        wu = _pad_last(wu.astype(BF16), D_FF_PAD)
        wd = jnp.pad(wd.astype(BF16), ((0, 0), (0, D_FF_PAD - D_FF), (0, 0)))
        return wg, wu, wd

    wa = ffn_weights(ffn_a_w_gate, ffn_a_w_up, ffn_a_w_down)
    wb = ffn_weights(ffn_b_w_gate, ffn_b_w_up, ffn_b_w_down)
    w_in_p = _pad_last(w_in.astype(BF16), P_IN_PAD)
    w_o = w_out.astype(BF16)
    w_o1, w_o2, w_o3 = w_o[:, :HG_W], w_o[:, HG_W:HG_W + DF_W], w_o[:, HG_W + DF_W:]

    uq = mla_w_uq.reshape(depth, ML_Q_RANK, ML_HEADS, ML_DQK)
    wqn = uq[..., :ML_NOPE].reshape(depth, ML_Q_RANK, ML_HEADS * ML_NOPE).astype(BF16)
    wqr = _pad_heads(uq[..., ML_NOPE:].reshape(depth, ML_Q_RANK, ML_HEADS * ML_ROPE), ML_ROPE).astype(BF16)
    ukv = mla_w_ukv.reshape(depth, ML_KV_RANK, ML_HEADS, ML_NOPE + ML_DV)
    wkn = ukv[..., :ML_NOPE].reshape(depth, ML_KV_RANK, ML_HEADS * ML_NOPE).astype(BF16)
    wv = ukv[..., ML_NOPE:].reshape(depth, ML_KV_RANK, ML_HEADS * ML_DV).astype(BF16)

    lb_all = jnp.cumsum(jax.nn.softmax(hgrn_lb_logits.astype(F32), axis=0), axis=0)
    lb_all = lb_all - lb_all[0:1]

    hg_masks = jnp.asarray(_hgrn_level_masks())
    b0, b1, far = _bias_tiles(rel_bias, ATT_TILE)
    ml_mask = jnp.where(_chunk_mask_tile(ATT_TILE), 0.0, NEG_BIG).astype(F32)[None]
    cos, sin = _rope_tables(seq)
    ones_blk = jnp.asarray(np.kron(np.eye(DF_W // DF_DQK), np.ones((DF_DQK, DF_DQK))), BF16)

    def row3(a):
        return a.reshape(a.shape[0], 1, a.shape[1])

    g_a, g_mix, g_b = row3(ffn_a_norm), row3(mix_norm), row3(ffn_b_norm)
    g_qln, g_kvln = row3(mla_q_lora_norm), row3(mla_kv_lora_norm)

    xf = x.reshape(m, d)
    for l in range(depth):
        xf = _ffn(xf, g_a, *wa, l)

        proj = _proj(xf, g_mix, w_in_p, l)
        o_hg = _hgrn(proj, lb_all[l][None, :], hgrn_out_norm[l][None, :], hg_masks, batch, seq)

        lambda_init = 0.8 - 0.6 * math.exp(-0.3 * l)
        lam = (jnp.exp(jnp.sum(diff_lambda_q1[l].astype(F32) * diff_lambda_k1[l].astype(F32)))
               - jnp.exp(jnp.sum(diff_lambda_q2[l].astype(F32) * diff_lambda_k2[l].astype(F32)))
               + lambda_init)
        qn, kn, vb = _df_prep(proj, jnp.tile(diff_q_norm[l], DF_W // DF_DQK)[None, :],
                              jnp.tile(diff_k_norm[l], DF_W // DF_DQK)[None, :], ones_blk)
        o_df = _df_attn(qn, kn, vb, b0, b1, far, jnp.full((1, DF_DV), lam, F32),
                        diff_subln[l][None, :], batch, seq, lambda_init)

        qg, kg = mla_q_norm[l], mla_k_norm[l]
        qh, kh, vm = _ml_prep(proj, g_qln, g_kvln, wqn, wqr, wkn, wv,
                              qg[None, :ML_NOPE], _pad_last(qg[None, ML_NOPE:], LANES),
                              kg[None, :ML_NOPE], _pad_last(kg[None, ML_NOPE:], LANES),
                              cos, sin, seq, l)
        o_ml = _ml_attn(qh, kh, vm, ml_mask, batch, seq)

        xf = _out_proj(xf, o_hg, o_df, o_ml, w_o1, w_o2, w_o3, l)
        xf = _ffn(xf, g_b, *wb, l)
    return xf.reshape(batch, seq, d)
```

```python
import functools
import math

import jax
import jax.numpy as jnp
import numpy as np
from jax import lax
from jax.experimental import pallas as pl
from jax.experimental.pallas import tpu as pltpu

F32 = jnp.float32
BF16 = jnp.bfloat16

D_MODEL = 2048
DEPTH = 4
CHUNK = 64
D_FF = 5504
EPS = 1e-6
HG_HEADS = 6
HG_DK = 128
HG_DV = 128
HG_W = HG_HEADS * HG_DK
DF_HEADS = 4
DF_DQK = 64
DF_DV = 128
DF_W = DF_HEADS * DF_DV
ML_HEADS = 6
ML_Q_RANK = 512
ML_KV_RANK = 256
ML_NOPE = 128
ML_ROPE = 64
ML_DV = 128
ML_DQK = ML_NOPE + ML_ROPE
ML_W = ML_HEADS * ML_DV
ROPE_BASE = 10000.0
REL_BUCKETS = 32
REL_MAX_DIST = 128
P_IN = 4 * HG_W + 3 * DF_W + ML_Q_RANK + ML_KV_RANK + ML_ROPE

LANES = 128
ML_HEAD_PAD = 2 * LANES
FF_TILE = 512
FF_SUB = 256
D_FF_PAD = -(-D_FF // FF_TILE) * FF_TILE
P_IN_PAD = -(-P_IN // FF_TILE) * FF_TILE
ROW_TILE = 512
PROJ_ROWS = 1024
FFN_ROWS = 1024
HG_ROWS = 256
ATT_TILE = 512
ATT_ROWS = 256
LOG2E = math.log2(math.e)
NEG_BIG = -1e30
VMEM_LIMIT = 56 * 1024 * 1024

_HG_COL = 0
_DF_COL = (4 * HG_W) // DF_W
_MLQ_COL = (4 * HG_W + 3 * DF_W) // ML_Q_RANK
_MLKV_COL = (4 * HG_W + 3 * DF_W + ML_Q_RANK) // ML_KV_RANK
_MLR_COL = (4 * HG_W + 3 * DF_W + ML_Q_RANK + ML_KV_RANK) // LANES


def _dot(a, b):
    return jnp.dot(a, b, preferred_element_type=F32)


def _dot_nt(a, b):
    return lax.dot_general(a, b, (((1,), (1,)), ((), ())), preferred_element_type=F32)


def _dot_tn(a, b):
    return lax.dot_general(a, b, (((0,), (0,)), ((), ())), preferred_element_type=F32)


def _params(*sem):
    return pltpu.CompilerParams(dimension_semantics=sem, vmem_limit_bytes=VMEM_LIMIT)


def _rms_rows(x, gain):
    return x * lax.rsqrt(jnp.mean(x * x, axis=-1, keepdims=True) + EPS) * gain


def _ffn_kernel(x_ref, g_ref, wg_ref, wu_ref, wd_ref, o_ref, h_ref):
    @pl.when(pl.program_id(1) == 0)
    def _():
        x = x_ref[...]
        h_ref[...] = _rms_rows(x, g_ref[...]).astype(BF16)
        o_ref[...] = x

    h = h_ref[...]
    acts = []
    for c in range(0, FF_TILE, FF_SUB):
        a = _dot(h, wg_ref[:, c:c + FF_SUB])
        u = _dot(h, wu_ref[:, c:c + FF_SUB])
        acts.append(((0.5 * a) * jax.nn.sigmoid(a) * u).astype(BF16))
    o_ref[...] += _dot(jnp.concatenate(acts, axis=-1), wd_ref[...])


def _ffn(x, gains, wg, wu, wd, layer):
    m, d = x.shape
    return pl.pallas_call(
        _ffn_kernel,
        grid=(m // FFN_ROWS, wg.shape[-1] // FF_TILE),
        in_specs=[
            pl.BlockSpec((FFN_ROWS, d), lambda i, j: (i, 0)),
            pl.BlockSpec((None, 1, d), lambda i, j: (layer, 0, 0)),
            pl.BlockSpec((None, d, FF_TILE), lambda i, j: (layer, 0, j)),
            pl.BlockSpec((None, d, FF_TILE), lambda i, j: (layer, 0, j)),
            pl.BlockSpec((None, FF_TILE, d), lambda i, j: (layer, j, 0)),
        ],
        out_specs=pl.BlockSpec((FFN_ROWS, d), lambda i, j: (i, 0)),
        out_shape=jax.ShapeDtypeStruct((m, d), F32),
        scratch_shapes=[pltpu.VMEM((FFN_ROWS, d), BF16)],
        compiler_params=_params("parallel", "arbitrary"),
        name="ffn",
    )(x, gains, wg, wu, wd)


def _proj_kernel(x_ref, g_ref, w_ref, o_ref, h_ref):
    @pl.when(pl.program_id(1) == 0)
    def _():
        h_ref[...] = _rms_rows(x_ref[...], g_ref[...]).astype(BF16)

    o_ref[...] = _dot(h_ref[...], w_ref[...])


def _proj(x, gains, w, layer):
    m, d = x.shape
    n = w.shape[-1]
    return pl.pallas_call(
        _proj_kernel,
        grid=(m // PROJ_ROWS, n // FF_TILE),
        in_specs=[
            pl.BlockSpec((PROJ_ROWS, d), lambda i, j: (i, 0)),
            pl.BlockSpec((None, 1, d), lambda i, j: (layer, 0, 0)),
            pl.BlockSpec((None, d, FF_TILE), lambda i, j: (layer, 0, j)),
        ],
        out_specs=pl.BlockSpec((PROJ_ROWS, FF_TILE), lambda i, j: (i, j)),
        out_shape=jax.ShapeDtypeStruct((m, n), F32),
        scratch_shapes=[pltpu.VMEM((PROJ_ROWS, d), BF16)],
        compiler_params=_params("parallel", "arbitrary"),
        name="proj",
    )(x, gains, w)


def _hgrn_level_masks():
    t = np.arange(CHUNK)[:, None]
    s = np.arange(CHUNK)[None, :]
    levels = [(((t >> l) ^ (s >> l)) == 1) & (t > s) for l in range(6)]
    levels.append(t == s)
    return np.stack(levels).astype(np.float32)


def _hgrn_row_masks():
    t = np.arange(CHUNK)
    pats = [t % 2 == 1, t % 4 == 0, t % 4 != 0, t % 4 == 2, t % 4 == 3]
    pats += [(t >> l) & 1 == 1 for l in range(2, 6)]
    return np.broadcast_to(np.stack(pats)[:, :, None], (len(pats), CHUNK, LANES)).astype(np.float32)


_RM_ODD, _RM_K1, _RM_NK1, _RM_M2, _RM_M3, _RM_UPPER2 = 0, 1, 2, 3, 4, 5


def _block_bcast(p, level):
    n, w = p.shape
    half = 1 << level
    size = 2 * half
    assert size >= 8
    pieces = []
    for b in range(n // size):
        mrow = b * size + half - 1
        pieces.append(jnp.broadcast_to(p[mrow:mrow + 1, :], (size, w)))
    return jnp.concatenate(pieces, axis=0)


def _hgrn_kernel(q_ref, f_ref, i_ref, g_ref, lb_ref, og_ref, mask_ref, rm_ref, o_ref, st_ref):
    @pl.when(pl.program_id(1) == 0)
    def _():
        st_ref[...] = jnp.zeros_like(st_ref)

    og = og_ref[...]

    def head_chunk(rows, h):
        sl = slice(h * HG_DK, (h + 1) * HG_DK)
        lb = lb_ref[:, sl]
        q = q_ref[rows, sl] * (HG_DK ** -0.5)
        f = lb + (1.0 - lb) * jax.nn.sigmoid(f_ref[rows, sl])
        kk = 1.0 - f
        kb = kk.astype(BF16)
        v = i_ref[rows, sl].astype(BF16)
        gate = g_ref[rows, sl]
        p = jnp.log(f) * LOG2E
        q_dec = [f, None]
        k_dec = [None, rm_ref[_RM_K1] * pltpu.roll(f, CHUNK - 1, 0) + rm_ref[_RM_NK1]]
        p = p + rm_ref[_RM_ODD] * pltpu.roll(p, 1, 0)
        q_dec[1] = jnp.exp2(p)
        p = p + rm_ref[_RM_M2] * pltpu.roll(p, 1, 0) + rm_ref[_RM_M3] * pltpu.roll(p, 2, 0)
        for l in range(2, 6):
            tb = _block_bcast(p, l)
            q_dec.append(jnp.exp2(p))
            k_dec.append(jnp.exp2(jnp.minimum(tb - p, 0.0)))
            p = p + rm_ref[_RM_UPPER2 + l - 2] * tb
        last = p[CHUNK - 1:CHUNK, :]

        scores = mask_ref[6] * _dot_nt(q.astype(BF16), kb)
        for l in range(6):
            kl = kb if k_dec[l] is None else (kk * k_dec[l]).astype(BF16)
            scores += mask_ref[l] * _dot_nt((q * q_dec[l]).astype(BF16), kl)
        st = st_ref[h]
        o = _dot_nt((q * jnp.exp2(p)).astype(BF16), st.astype(BF16)) + _dot(scores.astype(BF16), v)
        k_out = (kk * jnp.exp2(last - p)).astype(BF16)
        st_ref[h] = st * jnp.exp2(last) + _dot_tn(v, k_out)
        o = _rms_rows(o, og) * (gate * jax.nn.sigmoid(gate))
        o_ref[rows, sl] = o.astype(o_ref.dtype)

    for c in range(q_ref.shape[0] // CHUNK):
        for h in range(HG_HEADS):
            head_chunk(slice(c * CHUNK, (c + 1) * CHUNK), h)


def _hgrn(proj, lb, og, masks, row_masks, batch, seq):
    m = proj.shape[0]
    ns = seq // HG_ROWS

    def col(c):
        return pl.BlockSpec((HG_ROWS, HG_W), lambda b, s: (b * ns + s, _HG_COL + c))

    return pl.pallas_call(
        _hgrn_kernel,
        grid=(batch, ns),
        in_specs=[col(0), col(1), col(2), col(3),
                  pl.BlockSpec((1, HG_W), lambda b, s: (0, 0)),
                  pl.BlockSpec((1, HG_DV), lambda b, s: (0, 0)),
                  pl.BlockSpec((7, CHUNK, CHUNK), lambda b, s: (0, 0, 0)),
                  pl.BlockSpec(row_masks.shape, lambda b, s: (0, 0, 0))],
        out_specs=pl.BlockSpec((HG_ROWS, HG_W), lambda b, s: (b * ns + s, 0)),
        out_shape=jax.ShapeDtypeStruct((m, HG_W), BF16),
        scratch_shapes=[pltpu.VMEM((HG_HEADS, HG_DV, HG_DK), F32)],
        compiler_params=_params("parallel", "arbitrary"),
        name="hgrn",
    )(proj, proj, proj, proj, lb, og, masks, row_masks)


def _group_sumsq(x, ones_blk):
    x2 = x * x
    hi = x2.astype(BF16)
    lo = (x2 - hi.astype(F32)).astype(BF16)
    return _dot(hi, ones_blk) + _dot(lo, ones_blk)


def _df_prep_kernel(q_ref, k_ref, v_ref, qg_ref, kg_ref, ones_ref, qo_ref, ko_ref, vo_ref):
    ones_blk = ones_ref[...]
    q = q_ref[...]
    k = k_ref[...]
    qn = q * lax.rsqrt(_group_sumsq(q, ones_blk) * (1.0 / DF_DQK) + EPS) * qg_ref[...]
    kn = k * lax.rsqrt(_group_sumsq(k, ones_blk) * (1.0 / DF_DQK) + EPS) * kg_ref[...]
    qo_ref[...] = (qn * (DF_DQK ** -0.5 * LOG2E)).astype(BF16)
    ko_ref[...] = kn.astype(BF16)
    vo_ref[...] = v_ref[...].astype(BF16)


def _df_prep(proj, qg, kg, ones_blk):
    m = proj.shape[0]

    def col(c):
        return pl.BlockSpec((ROW_TILE, DF_W), lambda i: (i, _DF_COL + c))

    row = pl.BlockSpec((ROW_TILE, DF_W), lambda i: (i, 0))
    vec = pl.BlockSpec((1, DF_W), lambda i: (0, 0))
    out = jax.ShapeDtypeStruct((m, DF_W), BF16)
    return pl.pallas_call(
        _df_prep_kernel,
        grid=(m // ROW_TILE,),
        in_specs=[col(0), col(1), col(2), vec, vec, pl.BlockSpec((DF_W, DF_W), lambda i: (0, 0))],
        out_specs=[row, row, row],
        out_shape=[out, out, out],
        compiler_params=_params("parallel"),
        name="df_prep",
    )(proj, proj, proj, qg, kg, ones_blk)


def _attn_kernel(*refs, n_maps, has_bias, finish):
    if has_bias:
        q_ref, k_ref, v_ref, b0_ref, b1_ref, far_ref, *rest = refs
    else:
        q_ref, k_ref, v_ref, b0_ref, *rest = refs
        b1_ref = far_ref = None
    *extra, o_ref, m_ref, l_ref, acc_ref, s_ref = rest
    i = pl.program_id(2)
    t = q_ref.shape[0]
    nc = t // LANES
    if n_maps == 2:
        q = q_ref[...]
        lane = lax.broadcasted_iota(jnp.int32, q.shape, 1)
        zero = jnp.zeros_like(q)
        qs = [jnp.where(lane < DF_DQK, q, zero), jnp.where(lane >= DF_DQK, q, zero)]
    else:
        qs = [q_ref[...]]

    m_ref[...] = jnp.full_like(m_ref, NEG_BIG)
    l_ref[...] = jnp.zeros_like(l_ref)
    acc_ref[...] = jnp.zeros_like(acc_ref)

    chains = [(mi, slice(r0, r0 + ATT_ROWS)) for mi in range(n_maps) for r0 in range(0, t, ATT_ROWS)]

    def key_rows(n):
        return pl.ds(pl.multiple_of(n * t, t), t)

    def visible(r, diag):
        return r.stop // LANES if diag else nc

    def logits(n, slot, diag=False):
        kt = k_ref[key_rows(n), :]
        for mi, r in chains:
            w = visible(r, diag) * LANES
            s_ref[slot, mi, r, :w] = _dot_nt(qs[mi][r], kt[:w])

    def consume(n, slot, bias_ref, const, diag=False):
        vt = v_ref[key_rows(n), :]
        for mi, r in chains:
            cols = [s_ref[slot, mi, r, c * LANES:(c + 1) * LANES] for c in range(visible(r, diag))]
            if bias_ref is not None:
                cols = [x + bias_ref[0, r, c * LANES:(c + 1) * LANES] for c, x in enumerate(cols)]
            m_prev = m_ref[mi, r]
            row_max = jnp.max(functools.reduce(jnp.maximum, cols), axis=-1, keepdims=True)
            if const is None:
                m_new = jnp.maximum(m_prev, row_max)
                shift = m_new
            else:
                m_new = jnp.maximum(m_prev, row_max + const)
                shift = m_new - const
            alpha = jnp.exp2(m_prev - m_new)
            ps = [jnp.exp2(x - shift) for x in cols]
            l_ref[mi, r] = alpha * l_ref[mi, r] + functools.reduce(jnp.add, ps)
            p = jnp.concatenate([x.astype(BF16) for x in ps], axis=-1)
            acc_ref[mi, r] = alpha * acc_ref[mi, r] + _dot(p, vt[:p.shape[1]])
            m_ref[mi, r] = m_new

    far = far_ref[0, 0:1, 0:1] if has_bias else None
    prev_bias = b1_ref if has_bias else None

    n_far = jnp.maximum(i - 1, 0)
    n_pairs = lax.shift_right_logical(n_far, 1)
    logits(0, 0)

    def pair_body(j, carry):
        n = 2 * j
        logits(n + 1, 1)
        consume(n, 0, None, far)
        logits(n + 2, 0)
        consume(n + 1, 1, None, far)
        return carry

    lax.fori_loop(0, n_pairs, pair_body, 0)

    @pl.when(i == 0)
    def _():
        consume(0, 0, b0_ref, None, diag=True)

    @pl.when((i >= 1) & ((n_far & 1) == 0))
    def _():
        logits(i, 1, diag=True)
        consume(i - 1, 0, prev_bias, None)
        consume(i, 1, b0_ref, None, diag=True)

    @pl.when((n_far & 1) == 1)
    def _():
        logits(i - 1, 1)
        consume(i - 2, 0, None, far)
        logits(i, 0, diag=True)
        consume(i - 1, 1, prev_bias, None)
        consume(i, 0, b0_ref, None, diag=True)

    outs = [acc_ref[mi] / jnp.sum(l_ref[mi], axis=-1, keepdims=True) for mi in range(n_maps)]
    o_ref[...] = finish(outs, *extra).astype(o_ref.dtype)


def _df_finish(outs, lam_ref, sg_ref, *, lambda_init):
    o = outs[0] - lam_ref[...] * outs[1]
    return _rms_rows(o, sg_ref[...]) * (1.0 - lambda_init)


def _ml_finish(outs):
    return outs[0]


def _df_attn(qn, kn, vb, b0, b1, far, lam, sg, batch, seq, lambda_init):
    m = qn.shape[0]
    t = ATT_TILE
    nq = seq // t
    kv = pl.BlockSpec((seq, DF_DV), lambda b, h, i: (b, h))
    tile = pl.BlockSpec((1, t, t), lambda b, h, i: (h, 0, 0))
    vec = pl.BlockSpec((1, DF_DV), lambda b, h, i: (0, 0))
    kern = functools.partial(_attn_kernel, n_maps=2, has_bias=True,
                             finish=functools.partial(_df_finish, lambda_init=lambda_init))
    return pl.pallas_call(
        kern,
        grid=(batch, DF_HEADS, nq),
        in_specs=[pl.BlockSpec((t, DF_DV), lambda b, h, i: (b * nq + i, h)), kv, kv, tile, tile,
                  pl.BlockSpec((1, 8, LANES), lambda b, h, i: (h, 0, 0)), vec, vec],
        out_specs=pl.BlockSpec((t, DF_DV), lambda b, h, i: (b * nq + i, h)),
        out_shape=jax.ShapeDtypeStruct((m, DF_W), BF16),
        scratch_shapes=[pltpu.VMEM((2, t, LANES), F32), pltpu.VMEM((2, t, LANES), F32),
                        pltpu.VMEM((2, t, DF_DV), F32), pltpu.VMEM((2, 2, t, t), F32)],
        compiler_params=_params("parallel", "parallel", "arbitrary"),
        name="df_attn",
    )(qn, kn, vb, b0, b1, far, lam, sg)


def _ml_attn(qh, kh, vb, mask, batch, seq):
    m = qh.shape[0]
    t = ATT_TILE
    nq = seq // t
    return pl.pallas_call(
        functools.partial(_attn_kernel, n_maps=1, has_bias=False, finish=_ml_finish),
        grid=(batch, ML_HEADS, nq),
        in_specs=[pl.BlockSpec((t, ML_HEAD_PAD), lambda b, h, i: (b * nq + i, h)),
                  pl.BlockSpec((seq, ML_HEAD_PAD), lambda b, h, i: (b, h)),
                  pl.BlockSpec((seq, ML_DV), lambda b, h, i: (b, h)),
                  pl.BlockSpec((1, t, t), lambda b, h, i: (0, 0, 0))],
        out_specs=pl.BlockSpec((t, ML_DV), lambda b, h, i: (b * nq + i, h)),
        out_shape=jax.ShapeDtypeStruct((m, ML_W), BF16),
        scratch_shapes=[pltpu.VMEM((1, t, LANES), F32), pltpu.VMEM((1, t, LANES), F32),
                        pltpu.VMEM((1, t, ML_DV), F32), pltpu.VMEM((2, 1, t, t), F32)],
        compiler_params=_params("parallel", "parallel", "arbitrary"),
        name="ml_attn",
    )(qh, kh, vb, mask)


def _rope_pad(x, cos, sin):
    return x * cos + pltpu.roll(x, LANES // 2, 1) * sin


def _ml_prep_kernel(cq_ref, ckv_ref, kr_ref, qln_ref, kvln_ref, wqn_ref, wqr_ref, wkn_ref, wv_ref,
                    qgn_ref, qgr_ref, kgn_ref, kgr_ref, cos_ref, sin_ref, qo_ref, ko_ref, vo_ref):
    cos = cos_ref[...]
    sin = sin_ref[...]
    cq = _rms_rows(cq_ref[...], qln_ref[...]).astype(BF16)
    ckv = _rms_rows(ckv_ref[...], kvln_ref[...]).astype(BF16)
    q_nope = _dot(cq, wqn_ref[...])
    q_rope = _dot(cq, wqr_ref[...])
    k_nope = _dot(ckv, wkn_ref[...])
    vo_ref[...] = _dot(ckv, wv_ref[...]).astype(BF16)
    k_rope = kr_ref[...]
    kr_sq = k_rope * k_rope
    kr_rot = _rope_pad(k_rope * kgr_ref[...], cos, sin)
    qgn, qgr, kgn = qgn_ref[...], qgr_ref[...], kgn_ref[...]
    for h in range(ML_HEADS):
        sl = slice(h * LANES, (h + 1) * LANES)
        qn, qr, kn = q_nope[:, sl], q_rope[:, sl], k_nope[:, sl]
        q_ss = jnp.sum(qn * qn + qr * qr, axis=-1, keepdims=True)
        q_rs = lax.rsqrt(q_ss * (1.0 / ML_DQK) + EPS) * (ML_DQK ** -0.5 * LOG2E)
        k_ss = jnp.sum(kn * kn + kr_sq, axis=-1, keepdims=True)
        k_rs = lax.rsqrt(k_ss * (1.0 / ML_DQK) + EPS)
        base = h * ML_HEAD_PAD
        qo_ref[:, base:base + LANES] = (qn * q_rs * qgn).astype(BF16)
        qo_ref[:, base + LANES:base + 2 * LANES] = (_rope_pad(qr * qgr, cos, sin) * q_rs).astype(BF16)
        ko_ref[:, base:base + LANES] = (kn * k_rs * kgn).astype(BF16)
        ko_ref[:, base + LANES:base + 2 * LANES] = (kr_rot * k_rs).astype(BF16)


def _ml_prep(proj, qln, kvln, wqn, wqr, wkn, wv, qgn, qgr, kgn, kgr, cos, sin, seq, layer):
    m = proj.shape[0]
    npos = seq // ROW_TILE

    def full(a):
        nd = a.ndim
        if nd == 3:
            return pl.BlockSpec((None,) + a.shape[1:], lambda i: (layer, 0, 0))
        return pl.BlockSpec(a.shape, lambda i: (0, 0))

    pos = pl.BlockSpec((ROW_TILE, LANES), lambda i: (i % npos, 0))
    hp = ML_HEADS * ML_HEAD_PAD
    return pl.pallas_call(
        _ml_prep_kernel,
        grid=(m // ROW_TILE,),
        in_specs=[pl.BlockSpec((ROW_TILE, ML_Q_RANK), lambda i: (i, _MLQ_COL)),
                  pl.BlockSpec((ROW_TILE, ML_KV_RANK), lambda i: (i, _MLKV_COL)),
                  pl.BlockSpec((ROW_TILE, LANES), lambda i: (i, _MLR_COL)),
                  full(qln), full(kvln), full(wqn), full(wqr), full(wkn), full(wv),
                  full(qgn), full(qgr), full(kgn), full(kgr), pos, pos],
        out_specs=[pl.BlockSpec((ROW_TILE, hp), lambda i: (i, 0)),
                   pl.BlockSpec((ROW_TILE, hp), lambda i: (i, 0)),
                   pl.BlockSpec((ROW_TILE, ML_W), lambda i: (i, 0))],
        out_shape=[jax.ShapeDtypeStruct((m, hp), BF16), jax.ShapeDtypeStruct((m, hp), BF16),
                   jax.ShapeDtypeStruct((m, ML_W), BF16)],
        compiler_params=_params("parallel"),
        name="ml_prep",
    )(proj, proj, proj, qln, kvln, wqn, wqr, wkn, wv, qgn, qgr, kgn, kgr, cos, sin)


def _out_kernel(x_ref, hg_ref, df_ref, ml_ref, w1_ref, w2_ref, w3_ref, o_ref):
    o_ref[...] = (x_ref[...] + _dot(hg_ref[...], w1_ref[...]) + _dot(df_ref[...], w2_ref[...])
                  + _dot(ml_ref[...], w3_ref[...]))


def _out_proj(x, o_hg, o_df, o_ml, w1, w2, w3, layer):
    m, d = x.shape

    def rows(w):
        return pl.BlockSpec((ROW_TILE, w), lambda i: (i, 0))

    def wspec(w):
        return pl.BlockSpec((None,) + w.shape[1:], lambda i: (layer, 0, 0))

    return pl.pallas_call(
        _out_kernel,
        grid=(m // ROW_TILE,),
        in_specs=[rows(d), rows(HG_W), rows(DF_W), rows(ML_W), wspec(w1), wspec(w2), wspec(w3)],
        out_specs=rows(d),
        out_shape=jax.ShapeDtypeStruct((m, d), F32),
        compiler_params=_params("parallel"),
        name="out_proj",
    )(x, o_hg, o_df, o_ml, w1, w2, w3)


def _t5_bucket(rel):
    half = REL_BUCKETS // 2
    max_exact = half // 2
    ret = (rel > 0).astype(jnp.int32) * half
    n = jnp.abs(rel)
    large = max_exact + (jnp.log(jnp.maximum(n, 1).astype(F32) / max_exact)
                         / math.log(REL_MAX_DIST / max_exact) * (half - max_exact)).astype(jnp.int32)
    large = jnp.minimum(large, half - 1)
    return ret + jnp.where(n < max_exact, n, large)


def _chunk_mask_tile(t):
    ii = jnp.arange(t)
    return (ii[None, :] // CHUNK) <= (ii[:, None] // CHUNK)


def _bias_tiles(rel_bias, t):
    assert t >= REL_MAX_DIST and t % CHUNK == 0
    table = rel_bias.astype(F32)

    def lookup(bucket):
        out = jnp.zeros((DF_HEADS,) + bucket.shape, F32)
        for b in range(REL_BUCKETS):
            out = out + jnp.where(bucket[None] == b, table[b][:, None, None], 0.0)
        return out

    ii = jnp.arange(t)
    rel = ii[None, :] - ii[:, None]
    b0 = jnp.where(_chunk_mask_tile(t)[None], lookup(_t5_bucket(rel)) * LOG2E, NEG_BIG)
    b1 = lookup(_t5_bucket(rel - t)) * LOG2E
    far = table[_t5_bucket(jnp.full((), -2 * t, jnp.int32))] * LOG2E
    far = jnp.broadcast_to(far[:, None, None], (DF_HEADS, 8, LANES))
    return b0, b1, far


def _rope_tables(seq):
    r = ML_ROPE
    freqs = ROPE_BASE ** (-jnp.arange(0, r, 2, dtype=F32) / r)
    ang = jnp.arange(seq).astype(F32)[:, None] * freqs[None, :]
    cos, sin = jnp.cos(ang), jnp.sin(ang)
    return (_spread_rope(jnp.concatenate([cos, cos], axis=-1)),
            _spread_rope(jnp.concatenate([-sin, sin], axis=-1)))


def _spread_rope(a):
    half = ML_ROPE // 2
    z = jnp.zeros(a.shape[:-1] + (LANES // 2 - half,), a.dtype)
    return jnp.concatenate([a[..., :half], z, a[..., half:], z], axis=-1)


def _pad_axis(a, n, axis):
    shape = list(a.shape)
    shape[axis] = n - a.shape[axis]
    return jnp.concatenate([a, jnp.zeros(shape, a.dtype)], axis=axis)


def _pad_last(a, n):
    return _pad_axis(a, n, a.ndim - 1)


def _spread_rope_heads(a):
    lead = a.shape[:-1]
    return _spread_rope(a.reshape(lead + (ML_HEADS, ML_ROPE))).reshape(lead + (ML_HEADS * LANES,))


def kernel(x, ffn_a_norm, ffn_a_w_gate, ffn_a_w_up, ffn_a_w_down, mix_norm, w_in, w_out, hgrn_lb_logits, hgrn_out_norm, diff_q_norm, diff_k_norm, diff_lambda_q1, diff_lambda_k1, diff_lambda_q2, diff_lambda_k2, diff_subln, rel_bias, mla_q_lora_norm, mla_w_uq, mla_kv_lora_norm, mla_w_ukv, mla_q_norm, mla_k_norm, ffn_b_norm, ffn_b_w_gate, ffn_b_w_up, ffn_b_w_down):
    batch, seq, d = x.shape
    depth = w_in.shape[0]
    m = batch * seq
    assert m % PROJ_ROWS == 0 and m % FFN_ROWS == 0
    assert seq % ROW_TILE == 0 and seq % ATT_TILE == 0 and seq % HG_ROWS == 0

    def ffn_weights(wg, wu, wd):
        return (_pad_last(wg.astype(BF16), D_FF_PAD), _pad_last(wu.astype(BF16), D_FF_PAD),
                _pad_axis(wd.astype(BF16), D_FF_PAD, 1))

    wa = ffn_weights(ffn_a_w_gate, ffn_a_w_up, ffn_a_w_down)
    wb = ffn_weights(ffn_b_w_gate, ffn_b_w_up, ffn_b_w_down)
    w_in_b = w_in.astype(BF16)
    w_in_p = _pad_last(jnp.concatenate([w_in_b[..., :P_IN - ML_ROPE], _spread_rope(w_in_b[..., P_IN - ML_ROPE:])],
                                       axis=-1), P_IN_PAD)
    w_o = w_out.astype(BF16)
    w_o1, w_o2, w_o3 = w_o[:, :HG_W], w_o[:, HG_W:HG_W + DF_W], w_o[:, HG_W + DF_W:]

    uq = mla_w_uq.reshape(depth, ML_Q_RANK, ML_HEADS, ML_DQK)
    wqn = uq[..., :ML_NOPE].reshape(depth, ML_Q_RANK, ML_HEADS * ML_NOPE).astype(BF16)
    wqr = _spread_rope_heads(uq[..., ML_NOPE:].reshape(depth, ML_Q_RANK, ML_HEADS * ML_ROPE)).astype(BF16)
    ukv = mla_w_ukv.reshape(depth, ML_KV_RANK, ML_HEADS, ML_NOPE + ML_DV)
    wkn = ukv[..., :ML_NOPE].reshape(depth, ML_KV_RANK, ML_HEADS * ML_NOPE).astype(BF16)
    wv = ukv[..., ML_NOPE:].reshape(depth, ML_KV_RANK, ML_HEADS * ML_DV).astype(BF16)

    lb_all = jnp.cumsum(jax.nn.softmax(hgrn_lb_logits.astype(F32), axis=0), axis=0)
    lb_all = lb_all - lb_all[0:1]

    hg_masks = jnp.asarray(_hgrn_level_masks())
    hg_row_masks = jnp.asarray(_hgrn_row_masks())
    b0, b1, far = _bias_tiles(rel_bias, ATT_TILE)
    ml_mask = jnp.where(_chunk_mask_tile(ATT_TILE), 0.0, NEG_BIG).astype(F32)[None]
    cos, sin = _rope_tables(seq)
    ones_blk = jnp.asarray(np.kron(np.eye(DF_W // DF_DQK), np.ones((DF_DQK, DF_DQK))), BF16)

    def row3(a):
        return a.reshape(a.shape[0], 1, a.shape[1])

    g_a, g_mix, g_b = row3(ffn_a_norm), row3(mix_norm), row3(ffn_b_norm)
    g_qln, g_kvln = row3(mla_q_lora_norm), row3(mla_kv_lora_norm)

    xf = x.reshape(m, d)
    for l in range(depth):
        xf = _ffn(xf, g_a, *wa, l)

        proj = _proj(xf, g_mix, w_in_p, l)
        o_hg = _hgrn(proj, lb_all[l][None, :], hgrn_out_norm[l][None, :], hg_masks, hg_row_masks,
                     batch, seq)

        lambda_init = 0.8 - 0.6 * math.exp(-0.3 * l)
        lam = (jnp.exp(jnp.sum(diff_lambda_q1[l].astype(F32) * diff_lambda_k1[l].astype(F32)))
               - jnp.exp(jnp.sum(diff_lambda_q2[l].astype(F32) * diff_lambda_k2[l].astype(F32)))
               + lambda_init)
        qn, kn, vb = _df_prep(proj, jnp.tile(diff_q_norm[l], DF_W // DF_DQK)[None, :],
                              jnp.tile(diff_k_norm[l], DF_W // DF_DQK)[None, :], ones_blk)
        o_df = _df_attn(qn, kn, vb, b0, b1, far, jnp.full((1, DF_DV), lam, F32),
                        diff_subln[l][None, :], batch, seq, lambda_init)

        qg, kg = mla_q_norm[l], mla_k_norm[l]
        qh, kh, vm = _ml_prep(proj, g_qln, g_kvln, wqn, wqr, wkn, wv,
                              qg[None, :ML_NOPE], _spread_rope(qg[None, ML_NOPE:]),
                              kg[None, :ML_NOPE], _spread_rope(kg[None, ML_NOPE:]),
                              cos, sin, seq, l)
        o_ml = _ml_attn(qh, kh, vm, ml_mask, batch, seq)

        xf = _out_proj(xf, o_hg, o_df, o_ml, w_o1, w_o2, w_o3, l)
        xf = _ffn(xf, g_b, *wb, l)
    return xf.reshape(batch, seq, d)
```

```python
import functools
import math

import jax
import jax.numpy as jnp
import numpy as np
from jax import lax
from jax.experimental import pallas as pl
from jax.experimental.pallas import tpu as pltpu

F32 = jnp.float32
BF16 = jnp.bfloat16

D_MODEL = 2048
DEPTH = 4
CHUNK = 64
D_FF = 5504
EPS = 1e-6
HG_HEADS = 6
HG_DK = 128
HG_DV = 128
HG_W = HG_HEADS * HG_DK
DF_HEADS = 4
DF_DQK = 64
DF_DV = 128
DF_W = DF_HEADS * DF_DV
ML_HEADS = 6
ML_Q_RANK = 512
ML_KV_RANK = 256
ML_NOPE = 128
ML_ROPE = 64
ML_DV = 128
ML_DQK = ML_NOPE + ML_ROPE
ML_W = ML_HEADS * ML_DV
ROPE_BASE = 10000.0
REL_BUCKETS = 32
REL_MAX_DIST = 128
P_IN = 4 * HG_W + 3 * DF_W + ML_Q_RANK + ML_KV_RANK + ML_ROPE

LANES = 128
ML_HEAD_PAD = 2 * LANES
FF_TILE = 512
FF_SUB = 256
D_FF_PAD = -(-D_FF // FF_TILE) * FF_TILE
P_IN_PAD = -(-P_IN // FF_TILE) * FF_TILE
ROW_TILE = 512
PROJ_ROWS = 1024
FFN_ROWS = 1024
CAST_ROWS = 256
HG_ROWS = 256
ATT_TILE = 512
ATT_ROWS = 256
LOG2E = math.log2(math.e)
NEG_BIG = -1e30
VMEM_LIMIT = 56 * 1024 * 1024

_HG_COL = 0
_DF_COL = (4 * HG_W) // DF_W
_MLQ_COL = (4 * HG_W + 3 * DF_W) // ML_Q_RANK
_MLKV_COL = (4 * HG_W + 3 * DF_W + ML_Q_RANK) // ML_KV_RANK
_MLR_COL = (4 * HG_W + 3 * DF_W + ML_Q_RANK + ML_KV_RANK) // LANES


def _dot(a, b):
    return jnp.dot(a, b, preferred_element_type=F32)


def _dot_nt(a, b):
    return lax.dot_general(a, b, (((1,), (1,)), ((), ())), preferred_element_type=F32)


def _dot_tn(a, b):
    return lax.dot_general(a, b, (((0,), (0,)), ((), ())), preferred_element_type=F32)


def _params(*sem):
    return pltpu.CompilerParams(dimension_semantics=sem, vmem_limit_bytes=VMEM_LIMIT)


def _rms_rows(x, gain):
    return x * lax.rsqrt(jnp.mean(x * x, axis=-1, keepdims=True) + EPS) * gain


def _cast_pad_kernel(w_ref, o_ref):
    f = w_ref.shape[-1]
    o_ref[:, :f] = w_ref[...].astype(BF16)
    o_ref[:, f:] = jnp.zeros((o_ref.shape[0], o_ref.shape[1] - f), BF16)


def _cast_pad_cols(w, n):
    depth, d, f = w.shape
    assert f % LANES == 0 and n % LANES == 0 and d % CAST_ROWS == 0
    return pl.pallas_call(
        _cast_pad_kernel,
        grid=(depth, d // CAST_ROWS),
        in_specs=[pl.BlockSpec((None, CAST_ROWS, f), lambda l, i: (l, i, 0))],
        out_specs=pl.BlockSpec((None, CAST_ROWS, n), lambda l, i: (l, i, 0)),
        out_shape=jax.ShapeDtypeStruct((depth, d, n), BF16),
        compiler_params=_params("parallel", "parallel"),
        name="cast_pad",
    )(w)


def _ffn_kernel(x_ref, g_ref, wg_ref, wu_ref, wd_ref, o_ref, h_ref):
    @pl.when(pl.program_id(1) == 0)
    def _():
        x = x_ref[...]
        h_ref[...] = _rms_rows(x, g_ref[...]).astype(BF16)
        o_ref[...] = x

    h = h_ref[...]
    acts = []
    for c in range(0, FF_TILE, FF_SUB):
        a = _dot(h, wg_ref[:, c:c + FF_SUB])
        u = _dot(h, wu_ref[:, c:c + FF_SUB])
        acts.append(((0.5 * a) * jax.nn.sigmoid(a) * u).astype(BF16))
    o_ref[...] += _dot(jnp.concatenate(acts, axis=-1), wd_ref[...])


def _ffn(x, gains, wg, wu, wd, layer):
    m, d = x.shape
    return pl.pallas_call(
        _ffn_kernel,
        grid=(m // FFN_ROWS, wg.shape[-1] // FF_TILE),
        in_specs=[
            pl.BlockSpec((FFN_ROWS, d), lambda i, j: (i, 0)),
            pl.BlockSpec((None, 1, d), lambda i, j: (layer, 0, 0)),
            pl.BlockSpec((None, d, FF_TILE), lambda i, j: (layer, 0, j)),
            pl.BlockSpec((None, d, FF_TILE), lambda i, j: (layer, 0, j)),
            pl.BlockSpec((None, FF_TILE, d), lambda i, j: (layer, j, 0)),
        ],
        out_specs=pl.BlockSpec((FFN_ROWS, d), lambda i, j: (i, 0)),
        out_shape=jax.ShapeDtypeStruct((m, d), F32),
        scratch_shapes=[pltpu.VMEM((FFN_ROWS, d), BF16)],
        compiler_params=_params("parallel", "arbitrary"),
        name="ffn",
    )(x, gains, wg, wu, wd)


def _proj_kernel(x_ref, g_ref, w_ref, o_ref, h_ref):
    @pl.when(pl.program_id(1) == 0)
    def _():
        h_ref[...] = _rms_rows(x_ref[...], g_ref[...]).astype(BF16)

    o_ref[...] = _dot(h_ref[...], w_ref[...])


def _proj(x, gains, w, layer):
    m, d = x.shape
    n = w.shape[-1]
    return pl.pallas_call(
        _proj_kernel,
        grid=(m // PROJ_ROWS, n // FF_TILE),
        in_specs=[
            pl.BlockSpec((PROJ_ROWS, d), lambda i, j: (i, 0)),
            pl.BlockSpec((None, 1, d), lambda i, j: (layer, 0, 0)),
            pl.BlockSpec((None, d, FF_TILE), lambda i, j: (layer, 0, j)),
        ],
        out_specs=pl.BlockSpec((PROJ_ROWS, FF_TILE), lambda i, j: (i, j)),
        out_shape=jax.ShapeDtypeStruct((m, n), F32),
        scratch_shapes=[pltpu.VMEM((PROJ_ROWS, d), BF16)],
        compiler_params=_params("parallel", "arbitrary"),
        name="proj",
    )(x, gains, w)


def _hgrn_level_masks():
    t = np.arange(CHUNK)[:, None]
    s = np.arange(CHUNK)[None, :]
    levels = [(((t >> l) ^ (s >> l)) == 1) & (t > s) for l in range(6)]
    levels.append(t == s)
    return np.stack(levels).astype(np.float32)


def _hgrn_row_masks():
    t = np.arange(CHUNK)
    pats = [t % 2 == 1, t % 4 == 0, t % 4 != 0, t % 4 == 2, t % 4 == 3]
    pats += [(t >> l) & 1 == 1 for l in range(2, 6)]
    return np.broadcast_to(np.stack(pats)[:, :, None], (len(pats), CHUNK, LANES)).astype(np.float32)


_RM_ODD, _RM_K1, _RM_NK1, _RM_M2, _RM_M3, _RM_UPPER2 = 0, 1, 2, 3, 4, 5


def _block_bcast(p, level):
    n, w = p.shape
    half = 1 << level
    size = 2 * half
    assert size >= 8
    pieces = []
    for b in range(n // size):
        mrow = b * size + half - 1
        pieces.append(jnp.broadcast_to(p[mrow:mrow + 1, :], (size, w)))
    return jnp.concatenate(pieces, axis=0)


def _hgrn_kernel(q_ref, f_ref, i_ref, g_ref, lb_ref, og_ref, mask_ref, rm_ref, o_ref, st_ref):
    @pl.when(pl.program_id(1) == 0)
    def _():
        st_ref[...] = jnp.zeros_like(st_ref)

    og = og_ref[...]

    def head_chunk(rows, h):
        sl = slice(h * HG_DK, (h + 1) * HG_DK)
        lb = lb_ref[:, sl]
        q = q_ref[rows, sl] * (HG_DK ** -0.5)
        f = lb + (1.0 - lb) * jax.nn.sigmoid(f_ref[rows, sl])
        kk = 1.0 - f
        kb = kk.astype(BF16)
        v = i_ref[rows, sl].astype(BF16)
        gate = g_ref[rows, sl]
        p = jnp.log(f) * LOG2E
        q_dec = [f, None]
        k_dec = [None, rm_ref[_RM_K1] * pltpu.roll(f, CHUNK - 1, 0) + rm_ref[_RM_NK1]]
        p = p + rm_ref[_RM_ODD] * pltpu.roll(p, 1, 0)
        q_dec[1] = jnp.exp2(p)
        p = p + rm_ref[_RM_M2] * pltpu.roll(p, 1, 0) + rm_ref[_RM_M3] * pltpu.roll(p, 2, 0)
        for l in range(2, 6):
            tb = _block_bcast(p, l)
            q_dec.append(jnp.exp2(p))
            k_dec.append(jnp.exp2(jnp.minimum(tb - p, 0.0)))
            p = p + rm_ref[_RM_UPPER2 + l - 2] * tb
        last = p[CHUNK - 1:CHUNK, :]

        scores = mask_ref[6] * _dot_nt(q.astype(BF16), kb)
        for l in range(6):
            kl = kb if k_dec[l] is None else (kk * k_dec[l]).astype(BF16)
            scores += mask_ref[l] * _dot_nt((q * q_dec[l]).astype(BF16), kl)
        st = st_ref[h]
        o = _dot_nt((q * jnp.exp2(p)).astype(BF16), st.astype(BF16)) + _dot(scores.astype(BF16), v)
        k_out = (kk * jnp.exp2(last - p)).astype(BF16)
        st_ref[h] = st * jnp.exp2(last) + _dot_tn(v, k_out)
        o = _rms_rows(o, og) * (gate * jax.nn.sigmoid(gate))
        o_ref[rows, sl] = o.astype(o_ref.dtype)

    for c in range(q_ref.shape[0] // CHUNK):
        for h in range(HG_HEADS):
            head_chunk(slice(c * CHUNK, (c + 1) * CHUNK), h)


def _hgrn(proj, lb, og, masks, row_masks, batch, seq):
    m = proj.shape[0]
    ns = seq // HG_ROWS

    def col(c):
        return pl.BlockSpec((HG_ROWS, HG_W), lambda b, s: (b * ns + s, _HG_COL + c))

    return pl.pallas_call(
        _hgrn_kernel,
        grid=(batch, ns),
        in_specs=[col(0), col(1), col(2), col(3),
                  pl.BlockSpec((1, HG_W), lambda b, s: (0, 0)),
                  pl.BlockSpec((1, HG_DV), lambda b, s: (0, 0)),
                  pl.BlockSpec((7, CHUNK, CHUNK), lambda b, s: (0, 0, 0)),
                  pl.BlockSpec(row_masks.shape, lambda b, s: (0, 0, 0))],
        out_specs=pl.BlockSpec((HG_ROWS, HG_W), lambda b, s: (b * ns + s, 0)),
        out_shape=jax.ShapeDtypeStruct((m, HG_W), BF16),
        scratch_shapes=[pltpu.VMEM((HG_HEADS, HG_DV, HG_DK), F32)],
        compiler_params=_params("parallel", "arbitrary"),
        name="hgrn",
    )(proj, proj, proj, proj, lb, og, masks, row_masks)


def _group_sumsq(x, ones_blk):
    x2 = x * x
    hi = x2.astype(BF16)
    lo = (x2 - hi.astype(F32)).astype(BF16)
    return _dot(hi, ones_blk) + _dot(lo, ones_blk)


def _df_prep_kernel(q_ref, k_ref, v_ref, qg_ref, kg_ref, ones_ref, qo_ref, ko_ref, vo_ref):
    ones_blk = ones_ref[...]
    q = q_ref[...]
    k = k_ref[...]
    qn = q * lax.rsqrt(_group_sumsq(q, ones_blk) * (1.0 / DF_DQK) + EPS) * qg_ref[...]
    kn = k * lax.rsqrt(_group_sumsq(k, ones_blk) * (1.0 / DF_DQK) + EPS) * kg_ref[...]
    qo_ref[...] = (qn * (DF_DQK ** -0.5 * LOG2E)).astype(BF16)
    ko_ref[...] = kn.astype(BF16)
    vo_ref[...] = v_ref[...].astype(BF16)


def _df_prep(proj, qg, kg, ones_blk):
    m = proj.shape[0]

    def col(c):
        return pl.BlockSpec((ROW_TILE, DF_W), lambda i: (i, _DF_COL + c))

    row = pl.BlockSpec((ROW_TILE, DF_W), lambda i: (i, 0))
    vec = pl.BlockSpec((1, DF_W), lambda i: (0, 0))
    out = jax.ShapeDtypeStruct((m, DF_W), BF16)
    return pl.pallas_call(
        _df_prep_kernel,
        grid=(m // ROW_TILE,),
        in_specs=[col(0), col(1), col(2), vec, vec, pl.BlockSpec((DF_W, DF_W), lambda i: (0, 0))],
        out_specs=[row, row, row],
        out_shape=[out, out, out],
        compiler_params=_params("parallel"),
        name="df_prep",
    )(proj, proj, proj, qg, kg, ones_blk)


def _attn_kernel(*refs, n_maps, has_bias, finish):
    if has_bias:
        q_ref, k_ref, v_ref, b0_ref, b1_ref, far_ref, *rest = refs
    else:
        q_ref, k_ref, v_ref, b0_ref, *rest = refs
        b1_ref = far_ref = None
    *extra, o_ref, m_ref, l_ref, acc_ref, s_ref, p_ref = rest
    i = pl.program_id(2)
    t = q_ref.shape[0]
    nc = t // LANES
    if n_maps == 2:
        q = q_ref[...]
        lane = lax.broadcasted_iota(jnp.int32, q.shape, 1)
        zero = jnp.zeros_like(q)
        qs = [jnp.where(lane < DF_DQK, q, zero), jnp.where(lane >= DF_DQK, q, zero)]
    else:
        qs = [q_ref[...]]

    m_ref[...] = jnp.full_like(m_ref, NEG_BIG)
    l_ref[...] = jnp.zeros_like(l_ref)
    acc_ref[...] = jnp.zeros_like(acc_ref)

    chains = [(mi, slice(r0, r0 + ATT_ROWS)) for mi in range(n_maps) for r0 in range(0, t, ATT_ROWS)]

    def key_rows(n):
        return pl.ds(pl.multiple_of(n * t, t), t)

    def visible(r, diag):
        return r.stop // LANES if diag else nc

    def logits(n, slot, diag=False):
        kt = k_ref[key_rows(n), :]
        for mi, r in chains:
            w = visible(r, diag) * LANES
            s_ref[slot, mi, r, :w] = _dot_nt(qs[mi][r], kt[:w])

    def consume(n, slot, bias_ref, const, diag=False):
        vt = v_ref[key_rows(n), :]
        for mi, r in chains:
            def col(c):
                x = s_ref[slot, mi, r, c * LANES:(c + 1) * LANES]
                return x if bias_ref is None else x + bias_ref[0, r, c * LANES:(c + 1) * LANES]

            n_cols = visible(r, diag)
            m_prev = m_ref[mi, r]
            tile_max = functools.reduce(jnp.maximum, [col(c) for c in range(n_cols)])
            row_max = jnp.max(tile_max, axis=-1, keepdims=True)
            if const is None:
                m_new = jnp.maximum(m_prev, row_max)
                shift = m_new
            else:
                m_new = jnp.maximum(m_prev, row_max + const)
                shift = m_new - const
            m_ref[mi, r] = m_new
            alpha = jnp.exp2(m_prev - m_new)
            ps = [jnp.exp2(col(c) - shift) for c in range(n_cols)]
            l_ref[mi, r] = alpha * l_ref[mi, r] + functools.reduce(jnp.add, ps)
            p = jnp.concatenate([x.astype(BF16) for x in ps], axis=-1)
            acc_ref[mi, r] = alpha * acc_ref[mi, r] + _dot(p, vt[:p.shape[1]])

    far = far_ref[0, 0:1, 0:1] if has_bias else None
    prev_bias = b1_ref if has_bias else None

    n_far = jnp.maximum(i - 1, 0)
    n_pairs = lax.shift_right_logical(n_far, 1)
    logits(0, 0)

    def pair_body(j, carry):
        n = 2 * j
        logits(n + 1, 1)
        consume(n, 0, None, far)
        logits(n + 2, 0)
        consume(n + 1, 1, None, far)
        return carry

    lax.fori_loop(0, n_pairs, pair_body, 0)

    @pl.when(i == 0)
    def _():
        consume(0, 0, b0_ref, None, diag=True)

    @pl.when((i >= 1) & ((n_far & 1) == 0))
    def _():
        logits(i, 1, diag=True)
        consume(i - 1, 0, prev_bias, None)
        consume(i, 1, b0_ref, None, diag=True)

    @pl.when((n_far & 1) == 1)
    def _():
        logits(i - 1, 1)
        consume(i - 2, 0, None, far)
        logits(i, 0, diag=True)
        consume(i - 1, 1, prev_bias, None)
        consume(i, 0, b0_ref, None, diag=True)

    outs = [acc_ref[mi] / jnp.sum(l_ref[mi], axis=-1, keepdims=True) for mi in range(n_maps)]
    o_ref[...] = finish(outs, *extra).astype(o_ref.dtype)


def _df_finish(outs, lam_ref, sg_ref, *, lambda_init):
    o = outs[0] - lam_ref[...] * outs[1]
    return _rms_rows(o, sg_ref[...]) * (1.0 - lambda_init)


def _ml_finish(outs):
    return outs[0]


def _df_attn(qn, kn, vb, b0, b1, far, lam, sg, batch, seq, lambda_init):
    m = qn.shape[0]
    t = ATT_TILE
    nq = seq // t
    kv = pl.BlockSpec((seq, DF_DV), lambda b, h, i: (b, h))
    tile = pl.BlockSpec((1, t, t), lambda b, h, i: (h, 0, 0))
    vec = pl.BlockSpec((1, DF_DV), lambda b, h, i: (0, 0))
    kern = functools.partial(_attn_kernel, n_maps=2, has_bias=True,
                             finish=functools.partial(_df_finish, lambda_init=lambda_init))
    return pl.pallas_call(
        kern,
        grid=(batch, DF_HEADS, nq),
        in_specs=[pl.BlockSpec((t, DF_DV), lambda b, h, i: (b * nq + i, h)), kv, kv, tile, tile,
                  pl.BlockSpec((1, 8, LANES), lambda b, h, i: (h, 0, 0)), vec, vec],
        out_specs=pl.BlockSpec((t, DF_DV), lambda b, h, i: (b * nq + i, h)),
        out_shape=jax.ShapeDtypeStruct((m, DF_W), BF16),
        scratch_shapes=[pltpu.VMEM((2, t, LANES), F32), pltpu.VMEM((2, t, LANES), F32),
                        pltpu.VMEM((2, t, DF_DV), F32), pltpu.VMEM((2, 2, t, t), F32), pltpu.VMEM((2, t, t), BF16)],
        compiler_params=_params("parallel", "parallel", "arbitrary"),
        name="df_attn",
    )(qn, kn, vb, b0, b1, far, lam, sg)


def _ml_attn(qh, kh, vb, mask, batch, seq):
    m = qh.shape[0]
    t = ATT_TILE
    nq = seq // t
    return pl.pallas_call(
        functools.partial(_attn_kernel, n_maps=1, has_bias=False, finish=_ml_finish),
        grid=(batch, ML_HEADS, nq),
        in_specs=[pl.BlockSpec((t, ML_HEAD_PAD), lambda b, h, i: (b * nq + i, h)),
                  pl.BlockSpec((seq, ML_HEAD_PAD), lambda b, h, i: (b, h)),
                  pl.BlockSpec((seq, ML_DV), lambda b, h, i: (b, h)),
                  pl.BlockSpec((1, t, t), lambda b, h, i: (0, 0, 0))],
        out_specs=pl.BlockSpec((t, ML_DV), lambda b, h, i: (b * nq + i, h)),
        out_shape=jax.ShapeDtypeStruct((m, ML_W), BF16),
        scratch_shapes=[pltpu.VMEM((1, t, LANES), F32), pltpu.VMEM((1, t, LANES), F32),
                        pltpu.VMEM((1, t, ML_DV), F32), pltpu.VMEM((2, 1, t, t), F32), pltpu.VMEM((1, t, t), BF16)],
        compiler_params=_params("parallel", "parallel", "arbitrary"),
        name="ml_attn",
    )(qh, kh, vb, mask)


def _rope_pad(x, cos, sin):
    return x * cos + pltpu.roll(x, LANES // 2, 1) * sin


def _ml_prep_kernel(cq_ref, ckv_ref, kr_ref, qln_ref, kvln_ref, wqn_ref, wqr_ref, wkn_ref, wv_ref,
                    qgn_ref, qgr_ref, kgn_ref, kgr_ref, cos_ref, sin_ref, qo_ref, ko_ref, vo_ref):
    cos = cos_ref[...]
    sin = sin_ref[...]
    cq = _rms_rows(cq_ref[...], qln_ref[...]).astype(BF16)
    ckv = _rms_rows(ckv_ref[...], kvln_ref[...]).astype(BF16)
    q_nope = _dot(cq, wqn_ref[...])
    q_rope = _dot(cq, wqr_ref[...])
    k_nope = _dot(ckv, wkn_ref[...])
    vo_ref[...] = _dot(ckv, wv_ref[...]).astype(BF16)
    k_raw = kr_ref[...]
    lane = lax.broadcasted_iota(jnp.int32, k_raw.shape, 1)
    k_lo = jnp.where(lane < ML_ROPE // 2, k_raw, 0.0)
    k_rope = k_lo + pltpu.roll(k_raw - k_lo, ML_ROPE // 2, 1)
    kr_sq = k_rope * k_rope
    kr_rot = _rope_pad(k_rope * kgr_ref[...], cos, sin)
    qgn, qgr, kgn = qgn_ref[...], qgr_ref[...], kgn_ref[...]
    for h in range(ML_HEADS):
        sl = slice(h * LANES, (h + 1) * LANES)
        qn, qr, kn = q_nope[:, sl], q_rope[:, sl], k_nope[:, sl]
        q_ss = jnp.sum(qn * qn + qr * qr, axis=-1, keepdims=True)
        q_rs = lax.rsqrt(q_ss * (1.0 / ML_DQK) + EPS) * (ML_DQK ** -0.5 * LOG2E)
        k_ss = jnp.sum(kn * kn + kr_sq, axis=-1, keepdims=True)
        k_rs = lax.rsqrt(k_ss * (1.0 / ML_DQK) + EPS)
        base = h * ML_HEAD_PAD
        qo_ref[:, base:base + LANES] = (qn * q_rs * qgn).astype(BF16)
        qo_ref[:, base + LANES:base + 2 * LANES] = (_rope_pad(qr * qgr, cos, sin) * q_rs).astype(BF16)
        ko_ref[:, base:base + LANES] = (kn * k_rs * kgn).astype(BF16)
        ko_ref[:, base + LANES:base + 2 * LANES] = (kr_rot * k_rs).astype(BF16)


def _ml_prep(proj, qln, kvln, wqn, wqr, wkn, wv, qgn, qgr, kgn, kgr, cos, sin, seq, layer):
    m = proj.shape[0]
    npos = seq // ROW_TILE

    def full(a):
        nd = a.ndim
        if nd == 3:
            return pl.BlockSpec((None,) + a.shape[1:], lambda i: (layer, 0, 0))
        return pl.BlockSpec(a.shape, lambda i: (0, 0))

    pos = pl.BlockSpec((ROW_TILE, LANES), lambda i: (i % npos, 0))
    hp = ML_HEADS * ML_HEAD_PAD
    return pl.pallas_call(
        _ml_prep_kernel,
        grid=(m // ROW_TILE,),
        in_specs=[pl.BlockSpec((ROW_TILE, ML_Q_RANK), lambda i: (i, _MLQ_COL)),
                  pl.BlockSpec((ROW_TILE, ML_KV_RANK), lambda i: (i, _MLKV_COL)),
                  pl.BlockSpec((ROW_TILE, LANES), lambda i: (i, _MLR_COL)),
                  full(qln), full(kvln), full(wqn), full(wqr), full(wkn), full(wv),
                  full(qgn), full(qgr), full(kgn), full(kgr), pos, pos],
        out_specs=[pl.BlockSpec((ROW_TILE, hp), lambda i: (i, 0)),
                   pl.BlockSpec((ROW_TILE, hp), lambda i: (i, 0)),
                   pl.BlockSpec((ROW_TILE, ML_W), lambda i: (i, 0))],
        out_shape=[jax.ShapeDtypeStruct((m, hp), BF16), jax.ShapeDtypeStruct((m, hp), BF16),
                   jax.ShapeDtypeStruct((m, ML_W), BF16)],
        compiler_params=_params("parallel"),
        name="ml_prep",
    )(proj, proj, proj, qln, kvln, wqn, wqr, wkn, wv, qgn, qgr, kgn, kgr, cos, sin)


def _out_kernel(x_ref, hg_ref, df_ref, ml_ref, w1_ref, w2_ref, w3_ref, o_ref):
    o_ref[...] = (x_ref[...] + _dot(hg_ref[...], w1_ref[...]) + _dot(df_ref[...], w2_ref[...])
                  + _dot(ml_ref[...], w3_ref[...]))


def _out_proj(x, o_hg, o_df, o_ml, w1, w2, w3, layer):
    m, d = x.shape

    def rows(w):
        return pl.BlockSpec((ROW_TILE, w), lambda i: (i, 0))

    def wspec(w):
        return pl.BlockSpec((None,) + w.shape[1:], lambda i: (layer, 0, 0))

    return pl.pallas_call(
        _out_kernel,
        grid=(m // ROW_TILE,),
        in_specs=[rows(d), rows(HG_W), rows(DF_W), rows(ML_W), wspec(w1), wspec(w2), wspec(w3)],
        out_specs=rows(d),
        out_shape=jax.ShapeDtypeStruct((m, d), F32),
        compiler_params=_params("parallel"),
        name="out_proj",
    )(x, o_hg, o_df, o_ml, w1, w2, w3)


def _t5_bucket(rel):
    half = REL_BUCKETS // 2
    max_exact = half // 2
    ret = (rel > 0).astype(jnp.int32) * half
    n = jnp.abs(rel)
    large = max_exact + (jnp.log(jnp.maximum(n, 1).astype(F32) / max_exact)
                         / math.log(REL_MAX_DIST / max_exact) * (half - max_exact)).astype(jnp.int32)
    large = jnp.minimum(large, half - 1)
    return ret + jnp.where(n < max_exact, n, large)


def _chunk_mask_tile(t):
    ii = jnp.arange(t)
    return (ii[None, :] // CHUNK) <= (ii[:, None] // CHUNK)


def _bias_tiles(rel_bias, t):
    assert t >= REL_MAX_DIST and t % CHUNK == 0
    table = rel_bias.astype(F32)

    def lookup(bucket):
        out = jnp.zeros((DF_HEADS,) + bucket.shape, F32)
        for b in range(REL_BUCKETS):
            out = out + jnp.where(bucket[None] == b, table[b][:, None, None], 0.0)
        return out

    ii = jnp.arange(t)
    rel = ii[None, :] - ii[:, None]
    b0 = jnp.where(_chunk_mask_tile(t)[None], lookup(_t5_bucket(rel)) * LOG2E, NEG_BIG)
    b1 = lookup(_t5_bucket(rel - t)) * LOG2E
    far = table[_t5_bucket(jnp.full((), -2 * t, jnp.int32))] * LOG2E
    far = jnp.broadcast_to(far[:, None, None], (DF_HEADS, 8, LANES))
    return b0, b1, far


def _rope_tables(seq):
    r = ML_ROPE
    freqs = ROPE_BASE ** (-jnp.arange(0, r, 2, dtype=F32) / r)
    ang = jnp.arange(seq).astype(F32)[:, None] * freqs[None, :]
    cos, sin = jnp.cos(ang), jnp.sin(ang)
    return (_spread_rope(jnp.concatenate([cos, cos], axis=-1)),
            _spread_rope(jnp.concatenate([-sin, sin], axis=-1)))


def _spread_rope(a):
    half = ML_ROPE // 2
    z = jnp.zeros(a.shape[:-1] + (LANES // 2 - half,), a.dtype)
    return jnp.concatenate([a[..., :half], z, a[..., half:], z], axis=-1)


def _pad_axis(a, n, axis):
    shape = list(a.shape)
    shape[axis] = n - a.shape[axis]
    return jnp.concatenate([a, jnp.zeros(shape, a.dtype)], axis=axis)


def _pad_last(a, n):
    return _pad_axis(a, n, a.ndim - 1)


def _spread_rope_heads(a):
    lead = a.shape[:-1]
    return _spread_rope(a.reshape(lead + (ML_HEADS, ML_ROPE))).reshape(lead + (ML_HEADS * LANES,))


def kernel(x, ffn_a_norm, ffn_a_w_gate, ffn_a_w_up, ffn_a_w_down, mix_norm, w_in, w_out, hgrn_lb_logits, hgrn_out_norm, diff_q_norm, diff_k_norm, diff_lambda_q1, diff_lambda_k1, diff_lambda_q2, diff_lambda_k2, diff_subln, rel_bias, mla_q_lora_norm, mla_w_uq, mla_kv_lora_norm, mla_w_ukv, mla_q_norm, mla_k_norm, ffn_b_norm, ffn_b_w_gate, ffn_b_w_up, ffn_b_w_down):
    batch, seq, d = x.shape
    depth = w_in.shape[0]
    m = batch * seq
    assert m % PROJ_ROWS == 0 and m % FFN_ROWS == 0
    assert seq % ROW_TILE == 0 and seq % ATT_TILE == 0 and seq % HG_ROWS == 0

    def ffn_weights(wg, wu, wd):
        return (_cast_pad_cols(wg, D_FF_PAD), _cast_pad_cols(wu, D_FF_PAD),
                _pad_axis(wd.astype(BF16), D_FF_PAD, 1))

    wa = ffn_weights(ffn_a_w_gate, ffn_a_w_up, ffn_a_w_down)
    wb = ffn_weights(ffn_b_w_gate, ffn_b_w_up, ffn_b_w_down)
    w_in_p = _pad_last(w_in.astype(BF16), P_IN_PAD)
    w_o = w_out.astype(BF16)
    w_o1, w_o2, w_o3 = w_o[:, :HG_W], w_o[:, HG_W:HG_W + DF_W], w_o[:, HG_W + DF_W:]

    uq = mla_w_uq.reshape(depth, ML_Q_RANK, ML_HEADS, ML_DQK)
    wqn = uq[..., :ML_NOPE].reshape(depth, ML_Q_RANK, ML_HEADS * ML_NOPE).astype(BF16)
    wqr = _spread_rope_heads(uq[..., ML_NOPE:].reshape(depth, ML_Q_RANK, ML_HEADS * ML_ROPE)).astype(BF16)
    ukv = mla_w_ukv.reshape(depth, ML_KV_RANK, ML_HEADS, ML_NOPE + ML_DV)
    wkn = ukv[..., :ML_NOPE].reshape(depth, ML_KV_RANK, ML_HEADS * ML_NOPE).astype(BF16)
    wv = ukv[..., ML_NOPE:].reshape(depth, ML_KV_RANK, ML_HEADS * ML_DV).astype(BF16)

    lb_all = jnp.cumsum(jax.nn.softmax(hgrn_lb_logits.astype(F32), axis=0), axis=0)
    lb_all = lb_all - lb_all[0:1]

    hg_masks = jnp.asarray(_hgrn_level_masks())
    hg_row_masks = jnp.asarray(_hgrn_row_masks())
    b0, b1, far = _bias_tiles(rel_bias, ATT_TILE)
    ml_mask = jnp.where(_chunk_mask_tile(ATT_TILE), 0.0, NEG_BIG).astype(F32)[None]
    cos, sin = _rope_tables(seq)
    ones_blk = jnp.asarray(np.kron(np.eye(DF_W // DF_DQK), np.ones((DF_DQK, DF_DQK))), BF16)

    def row3(a):
        return a.reshape(a.shape[0], 1, a.shape[1])

    g_a, g_mix, g_b = row3(ffn_a_norm), row3(mix_norm), row3(ffn_b_norm)
    g_qln, g_kvln = row3(mla_q_lora_norm), row3(mla_kv_lora_norm)

    xf = x.reshape(m, d)
    for l in range(depth):
        xf = _ffn(xf, g_a, *wa, l)

        proj = _proj(xf, g_mix, w_in_p, l)
        o_hg = _hgrn(proj, lb_all[l][None, :], hgrn_out_norm[l][None, :], hg_masks, hg_row_masks,
                     batch, seq)

        lambda_init = 0.8 - 0.6 * math.exp(-0.3 * l)
        lam = (jnp.exp(jnp.sum(diff_lambda_q1[l].astype(F32) * diff_lambda_k1[l].astype(F32)))
               - jnp.exp(jnp.sum(diff_lambda_q2[l].astype(F32) * diff_lambda_k2[l].astype(F32)))
               + lambda_init)
        qn, kn, vb = _df_prep(proj, jnp.tile(diff_q_norm[l], DF_W // DF_DQK)[None, :],
                              jnp.tile(diff_k_norm[l], DF_W // DF_DQK)[None, :], ones_blk)
        o_df = _df_attn(qn, kn, vb, b0, b1, far, jnp.full((1, DF_DV), lam, F32),
                        diff_subln[l][None, :], batch, seq, lambda_init)

        qg, kg = mla_q_norm[l], mla_k_norm[l]
        qh, kh, vm = _ml_prep(proj, g_qln, g_kvln, wqn, wqr, wkn, wv,
                              qg[None, :ML_NOPE], _spread_rope(qg[None, ML_NOPE:]),
                              kg[None, :ML_NOPE], _spread_rope(kg[None, ML_NOPE:]),
                              cos, sin, seq, l)
        o_ml = _ml_attn(qh, kh, vm, ml_mask, batch, seq)

        xf = _out_proj(xf, o_hg, o_df, o_ml, w_o1, w_o2, w_o3, l)
        xf = _ffn(xf, g_b, *wb, l)
    return xf.reshape(batch, seq, d)
```

```python
import functools
import math

import jax
import jax.numpy as jnp
import numpy as np
from jax import lax
from jax.experimental import pallas as pl
from jax.experimental.pallas import tpu as pltpu

F32 = jnp.float32
BF16 = jnp.bfloat16

D_MODEL = 2048
DEPTH = 4
CHUNK = 64
D_FF = 5504
EPS = 1e-6
HG_HEADS = 6
HG_DK = 128
HG_DV = 128
HG_W = HG_HEADS * HG_DK
DF_HEADS = 4
DF_DQK = 64
DF_DV = 128
DF_W = DF_HEADS * DF_DV
ML_HEADS = 6
ML_Q_RANK = 512
ML_KV_RANK = 256
ML_NOPE = 128
ML_ROPE = 64
ML_DV = 128
ML_DQK = ML_NOPE + ML_ROPE
ML_W = ML_HEADS * ML_DV
ROPE_BASE = 10000.0
REL_BUCKETS = 32
REL_MAX_DIST = 128
P_IN = 4 * HG_W + 3 * DF_W + ML_Q_RANK + ML_KV_RANK + ML_ROPE

LANES = 128
ML_HEAD_PAD = 2 * LANES
FF_TILE = 512
FF_SUB = 256
D_FF_PAD = -(-D_FF // FF_TILE) * FF_TILE
P_IN_PAD = -(-P_IN // FF_TILE) * FF_TILE
ROW_TILE = 1024
PROJ_ROWS = 1024
FFN_ROWS = 1024
CAST_ROWS = 256
HG_ROWS = 512
ATT_TILE = 512
ATT_ROWS = 256
LOG2E = math.log2(math.e)
NEG_BIG = -1e30
VMEM_LIMIT = 56 * 1024 * 1024

_HG_COL = 0
_DF_COL = (4 * HG_W) // DF_W
_MLQ_COL = (4 * HG_W + 3 * DF_W) // ML_Q_RANK
_MLKV_COL = (4 * HG_W + 3 * DF_W + ML_Q_RANK) // ML_KV_RANK
_MLR_COL = (4 * HG_W + 3 * DF_W + ML_Q_RANK + ML_KV_RANK) // LANES


def _dot(a, b):
    return jnp.dot(a, b, preferred_element_type=F32)


def _dot_nt(a, b):
    return lax.dot_general(a, b, (((1,), (1,)), ((), ())), preferred_element_type=F32)


def _dot_tn(a, b):
    return lax.dot_general(a, b, (((0,), (0,)), ((), ())), preferred_element_type=F32)


def _params(*sem):
    return pltpu.CompilerParams(dimension_semantics=sem, vmem_limit_bytes=VMEM_LIMIT)


def _rms_rows(x, gain):
    return x * lax.rsqrt(jnp.mean(x * x, axis=-1, keepdims=True) + EPS) * gain


def _cast_pad_kernel(w_ref, o_ref):
    rows, f = w_ref.shape
    fa = f // LANES * LANES
    o_ref[:, :fa] = w_ref[:, :fa].astype(BF16)
    if fa < f:
        tail = jnp.concatenate([w_ref[:, fa:], jnp.zeros((rows, fa + LANES - f), F32)], axis=-1)
        o_ref[:, fa:fa + LANES] = tail.astype(BF16)
        fa += LANES
    o_ref[:, fa:] = jnp.zeros((rows, o_ref.shape[1] - fa), BF16)


def _cast_pad_cols(w, n):
    depth, d, f = w.shape
    assert n % LANES == 0 and n > f and d % CAST_ROWS == 0
    return pl.pallas_call(
        _cast_pad_kernel,
        grid=(depth, d // CAST_ROWS),
        in_specs=[pl.BlockSpec((None, CAST_ROWS, f), lambda l, i: (l, i, 0))],
        out_specs=pl.BlockSpec((None, CAST_ROWS, n), lambda l, i: (l, i, 0)),
        out_shape=jax.ShapeDtypeStruct((depth, d, n), BF16),
        compiler_params=_params("parallel", "parallel"),
        name="cast_pad",
    )(w)


def _ffn_kernel(x_ref, g_ref, wg_ref, wu_ref, wd_ref, o_ref, h_ref):
    @pl.when(pl.program_id(1) == 0)
    def _():
        x = x_ref[...]
        h_ref[...] = _rms_rows(x, g_ref[...]).astype(BF16)
        o_ref[...] = x

    h = h_ref[...]
    acts = []
    for c in range(0, FF_TILE, FF_SUB):
        a = _dot(h, wg_ref[:, c:c + FF_SUB])
        u = _dot(h, wu_ref[:, c:c + FF_SUB])
        acts.append(((0.5 * a) * jax.nn.sigmoid(a) * u).astype(BF16))
    o_ref[...] += _dot(jnp.concatenate(acts, axis=-1), wd_ref[...])


def _ffn(x, gains, wg, wu, wd, layer):
    m, d = x.shape
    return pl.pallas_call(
        _ffn_kernel,
        grid=(m // FFN_ROWS, wg.shape[-1] // FF_TILE),
        in_specs=[
            pl.BlockSpec((FFN_ROWS, d), lambda i, j: (i, 0)),
            pl.BlockSpec((None, 1, d), lambda i, j: (layer, 0, 0)),
            pl.BlockSpec((None, d, FF_TILE), lambda i, j: (layer, 0, j)),
            pl.BlockSpec((None, d, FF_TILE), lambda i, j: (layer, 0, j)),
            pl.BlockSpec((None, FF_TILE, d), lambda i, j: (layer, j, 0)),
        ],
        out_specs=pl.BlockSpec((FFN_ROWS, d), lambda i, j: (i, 0)),
        out_shape=jax.ShapeDtypeStruct((m, d), F32),
        scratch_shapes=[pltpu.VMEM((FFN_ROWS, d), BF16)],
        compiler_params=_params("parallel", "arbitrary"),
        name="ffn",
    )(x, gains, wg, wu, wd)


def _proj_kernel(x_ref, g_ref, w_ref, o_ref, h_ref):
    @pl.when(pl.program_id(1) == 0)
    def _():
        h_ref[...] = _rms_rows(x_ref[...], g_ref[...]).astype(BF16)

    o_ref[...] = _dot(h_ref[...], w_ref[...])


def _proj(x, gains, w, layer):
    m, d = x.shape
    n = w.shape[-1]
    return pl.pallas_call(
        _proj_kernel,
        grid=(m // PROJ_ROWS, n // FF_TILE),
        in_specs=[
            pl.BlockSpec((PROJ_ROWS, d), lambda i, j: (i, 0)),
            pl.BlockSpec((None, 1, d), lambda i, j: (layer, 0, 0)),
            pl.BlockSpec((None, d, FF_TILE), lambda i, j: (layer, 0, j)),
        ],
        out_specs=pl.BlockSpec((PROJ_ROWS, FF_TILE), lambda i, j: (i, j)),
        out_shape=jax.ShapeDtypeStruct((m, n), F32),
        scratch_shapes=[pltpu.VMEM((PROJ_ROWS, d), BF16)],
        compiler_params=_params("parallel", "arbitrary"),
        name="proj",
    )(x, gains, w)


def _hgrn_level_masks():
    t = np.arange(CHUNK)[:, None]
    s = np.arange(CHUNK)[None, :]
    levels = [(((t >> l) ^ (s >> l)) == 1) & (t > s) for l in range(6)]
    levels.append(t == s)
    return np.stack(levels).astype(np.float32)


def _hgrn_row_masks():
    t = np.arange(CHUNK)
    pats = [t % 2 == 1, t % 4 == 0, t % 4 != 0, t % 4 == 2, t % 4 == 3]
    pats += [(t >> l) & 1 == 1 for l in range(2, 6)]
    return np.broadcast_to(np.stack(pats)[:, :, None], (len(pats), CHUNK, LANES)).astype(np.float32)


_RM_ODD, _RM_K1, _RM_NK1, _RM_M2, _RM_M3, _RM_UPPER2 = 0, 1, 2, 3, 4, 5


def _block_bcast(p, level):
    n, w = p.shape
    half = 1 << level
    size = 2 * half
    assert size >= 8
    pieces = []
    for b in range(n // size):
        mrow = b * size + half - 1
        pieces.append(jnp.broadcast_to(p[mrow:mrow + 1, :], (size, w)))
    return jnp.concatenate(pieces, axis=0)


def _hgrn_kernel(q_ref, f_ref, i_ref, g_ref, lb_ref, og_ref, mask_ref, rm_ref, o_ref, st_ref):
    @pl.when(pl.program_id(1) == 0)
    def _():
        st_ref[...] = jnp.zeros_like(st_ref)

    og = og_ref[...]

    def head_chunk(rows, h):
        sl = slice(h * HG_DK, (h + 1) * HG_DK)
        lb = lb_ref[:, sl]
        q = q_ref[rows, sl] * (HG_DK ** -0.5)
        f = lb + (1.0 - lb) * jax.nn.sigmoid(f_ref[rows, sl])
        kk = 1.0 - f
        kb = kk.astype(BF16)
        v = i_ref[rows, sl].astype(BF16)
        gate = g_ref[rows, sl]
        p = jnp.log(f) * LOG2E
        q_dec = [f, None]
        k_dec = [None, rm_ref[_RM_K1] * pltpu.roll(f, CHUNK - 1, 0) + rm_ref[_RM_NK1]]
        p = p + rm_ref[_RM_ODD] * pltpu.roll(p, 1, 0)
        q_dec[1] = jnp.exp2(p)
        p = p + rm_ref[_RM_M2] * pltpu.roll(p, 1, 0) + rm_ref[_RM_M3] * pltpu.roll(p, 2, 0)
        for l in range(2, 6):
            tb = _block_bcast(p, l)
            q_dec.append(jnp.exp2(p))
            k_dec.append(jnp.exp2(jnp.minimum(tb - p, 0.0)))
            p = p + rm_ref[_RM_UPPER2 + l - 2] * tb
        last = p[CHUNK - 1:CHUNK, :]

        scores = mask_ref[6] * _dot_nt(q.astype(BF16), kb)
        for l in range(6):
            kl = kb if k_dec[l] is None else (kk * k_dec[l]).astype(BF16)
            scores += mask_ref[l] * _dot_nt((q * q_dec[l]).astype(BF16), kl)
        st = st_ref[h]
        o = _dot_nt((q * jnp.exp2(p)).astype(BF16), st.astype(BF16)) + _dot(scores.astype(BF16), v)
        k_out = (kk * jnp.exp2(last - p)).astype(BF16)
        st_ref[h] = st * jnp.exp2(last) + _dot_tn(v, k_out)
        o = _rms_rows(o, og) * (gate * jax.nn.sigmoid(gate))
        o_ref[rows, sl] = o.astype(o_ref.dtype)

    for c in range(q_ref.shape[0] // CHUNK):
        for h in range(HG_HEADS):
            head_chunk(slice(c * CHUNK, (c + 1) * CHUNK), h)


def _hgrn(proj, lb, og, masks, row_masks, batch, seq):
    m = proj.shape[0]
    ns = seq // HG_ROWS

    def col(c):
        return pl.BlockSpec((HG_ROWS, HG_W), lambda b, s: (b * ns + s, _HG_COL + c))

    return pl.pallas_call(
        _hgrn_kernel,
        grid=(batch, ns),
        in_specs=[col(0), col(1), col(2), col(3),
                  pl.BlockSpec((1, HG_W), lambda b, s: (0, 0)),
                  pl.BlockSpec((1, HG_DV), lambda b, s: (0, 0)),
                  pl.BlockSpec((7, CHUNK, CHUNK), lambda b, s: (0, 0, 0)),
                  pl.BlockSpec(row_masks.shape, lambda b, s: (0, 0, 0))],
        out_specs=pl.BlockSpec((HG_ROWS, HG_W), lambda b, s: (b * ns + s, 0)),
        out_shape=jax.ShapeDtypeStruct((m, HG_W), BF16),
        scratch_shapes=[pltpu.VMEM((HG_HEADS, HG_DV, HG_DK), F32)],
        compiler_params=_params("parallel", "arbitrary"),
        name="hgrn",
    )(proj, proj, proj, proj, lb, og, masks, row_masks)


def _group_sumsq(x, ones_blk):
    x2 = x * x
    hi = x2.astype(BF16)
    lo = (x2 - hi.astype(F32)).astype(BF16)
    return _dot(hi, ones_blk) + _dot(lo, ones_blk)


def _df_prep_kernel(q_ref, k_ref, v_ref, qg_ref, kg_ref, ones_ref, qo_ref, ko_ref, vo_ref):
    ones_blk = ones_ref[...]
    q = q_ref[...]
    k = k_ref[...]
    qn = q * lax.rsqrt(_group_sumsq(q, ones_blk) * (1.0 / DF_DQK) + EPS) * qg_ref[...]
    kn = k * lax.rsqrt(_group_sumsq(k, ones_blk) * (1.0 / DF_DQK) + EPS) * kg_ref[...]
    qo_ref[...] = (qn * (DF_DQK ** -0.5 * LOG2E)).astype(BF16)
    ko_ref[...] = kn.astype(BF16)
    vo_ref[...] = v_ref[...].astype(BF16)


def _df_prep(proj, qg, kg, ones_blk):
    m = proj.shape[0]

    def col(c):
        return pl.BlockSpec((ROW_TILE, DF_W), lambda i: (i, _DF_COL + c))

    row = pl.BlockSpec((ROW_TILE, DF_W), lambda i: (i, 0))
    vec = pl.BlockSpec((1, DF_W), lambda i: (0, 0))
    out = jax.ShapeDtypeStruct((m, DF_W), BF16)
    return pl.pallas_call(
        _df_prep_kernel,
        grid=(m // ROW_TILE,),
        in_specs=[col(0), col(1), col(2), vec, vec, pl.BlockSpec((DF_W, DF_W), lambda i: (0, 0))],
        out_specs=[row, row, row],
        out_shape=[out, out, out],
        compiler_params=_params("parallel"),
        name="df_prep",
    )(proj, proj, proj, qg, kg, ones_blk)


def _attn_kernel(*refs, n_maps, has_bias, finish):
    if has_bias:
        q_ref, k_ref, v_ref, b0_ref, b1_ref, far_ref, *rest = refs
    else:
        q_ref, k_ref, v_ref, b0_ref, *rest = refs
        b1_ref = far_ref = None
    *extra, o_ref, m_ref, l_ref, acc_ref, s_ref, p_ref = rest
    i = pl.program_id(2)
    t = q_ref.shape[0]
    nc = t // LANES
    if n_maps == 2:
        q = q_ref[...]
        lane = lax.broadcasted_iota(jnp.int32, q.shape, 1)
        zero = jnp.zeros_like(q)
        qs = [jnp.where(lane < DF_DQK, q, zero), jnp.where(lane >= DF_DQK, q, zero)]
    else:
        qs = [q_ref[...]]
    q_all = qs[0] if n_maps == 1 else jnp.concatenate(qs, axis=0)

    m_ref[...] = jnp.full_like(m_ref, NEG_BIG)
    l_ref[...] = jnp.zeros_like(l_ref)
    acc_ref[...] = jnp.zeros_like(acc_ref)

    chains = [(mi, slice(r0, r0 + ATT_ROWS)) for mi in range(n_maps) for r0 in range(0, t, ATT_ROWS)]

    def key_rows(n):
        return pl.ds(pl.multiple_of(n * t, t), t)

    def visible(r, diag):
        return r.stop // LANES if diag else nc

    def logits(n, slot, diag=False):
        kt = k_ref[key_rows(n), :]
        if not diag:
            s = _dot_nt(q_all, kt)
            for mi in range(n_maps):
                s_ref[slot, mi] = s[mi * t:(mi + 1) * t]
            return
        for mi, r in chains:
            w = visible(r, diag) * LANES
            s_ref[slot, mi, r, :w] = _dot_nt(qs[mi][r], kt[:w])

    def consume(n, slot, bias_ref, const, diag=False):
        vt = v_ref[key_rows(n), :]
        for mi, r in chains:
            def col(c):
                x = s_ref[slot, mi, r, c * LANES:(c + 1) * LANES]
                return x if bias_ref is None else x + bias_ref[0, r, c * LANES:(c + 1) * LANES]

            n_cols = visible(r, diag)
            m_prev = m_ref[mi, r]
            tile_max = functools.reduce(jnp.maximum, [col(c) for c in range(n_cols)])
            row_max = jnp.max(tile_max, axis=-1, keepdims=True)
            if const is None:
                m_new = jnp.maximum(m_prev, row_max)
                shift = m_new
            else:
                m_new = jnp.maximum(m_prev, row_max + const)
                shift = m_new - const
            m_ref[mi, r] = m_new
            alpha = jnp.exp2(m_prev - m_new)
            ps = [jnp.exp2(col(c) - shift) for c in range(n_cols)]
            l_ref[mi, r] = alpha * l_ref[mi, r] + functools.reduce(jnp.add, ps)
            p = jnp.concatenate([x.astype(BF16) for x in ps], axis=-1)
            acc_ref[mi, r] = alpha * acc_ref[mi, r] + _dot(p, vt[:p.shape[1]])

    far = far_ref[0, 0:1, 0:1] if has_bias else None
    prev_bias = b1_ref if has_bias else None

    n_far = jnp.maximum(i - 1, 0)
    n_pairs = lax.shift_right_logical(n_far, 1)
    logits(0, 0)

    def pair_body(j, carry):
        n = 2 * j
        logits(n + 1, 1)
        consume(n, 0, None, far)
        logits(n + 2, 0)
        consume(n + 1, 1, None, far)
        return carry

    lax.fori_loop(0, n_pairs, pair_body, 0)

    @pl.when(i == 0)
    def _():
        consume(0, 0, b0_ref, None, diag=True)

    @pl.when((i >= 1) & ((n_far & 1) == 0))
    def _():
        logits(i, 1, diag=True)
        consume(i - 1, 0, prev_bias, None)
        consume(i, 1, b0_ref, None, diag=True)

    @pl.when((n_far & 1) == 1)
    def _():
        logits(i - 1, 1)
        consume(i - 2, 0, None, far)
        logits(i, 0, diag=True)
        consume(i - 1, 1, prev_bias, None)
        consume(i, 0, b0_ref, None, diag=True)

    outs = [acc_ref[mi] / jnp.sum(l_ref[mi], axis=-1, keepdims=True) for mi in range(n_maps)]
    o_ref[...] = finish(outs, *extra).astype(o_ref.dtype)


def _df_finish(outs, lam_ref, sg_ref, *, lambda_init):
    o = outs[0] - lam_ref[...] * outs[1]
    return _rms_rows(o, sg_ref[...]) * (1.0 - lambda_init)


def _ml_finish(outs):
    return outs[0]


def _df_attn(qn, kn, vb, b0, b1, far, lam, sg, batch, seq, lambda_init):
    m = qn.shape[0]
    t = ATT_TILE
    nq = seq // t
    kv = pl.BlockSpec((seq, DF_DV), lambda b, h, i: (b, h))
    tile = pl.BlockSpec((1, t, t), lambda b, h, i: (h, 0, 0))
    vec = pl.BlockSpec((1, DF_DV), lambda b, h, i: (0, 0))
    kern = functools.partial(_attn_kernel, n_maps=2, has_bias=True,
                             finish=functools.partial(_df_finish, lambda_init=lambda_init))
    return pl.pallas_call(
        kern,
        grid=(batch, DF_HEADS, nq),
        in_specs=[pl.BlockSpec((t, DF_DV), lambda b, h, i: (b * nq + i, h)), kv, kv, tile, tile,
                  pl.BlockSpec((1, 8, LANES), lambda b, h, i: (h, 0, 0)), vec, vec],
        out_specs=pl.BlockSpec((t, DF_DV), lambda b, h, i: (b * nq + i, h)),
        out_shape=jax.ShapeDtypeStruct((m, DF_W), BF16),
        scratch_shapes=[pltpu.VMEM((2, t, LANES), F32), pltpu.VMEM((2, t, LANES), F32),
                        pltpu.VMEM((2, t, DF_DV), F32), pltpu.VMEM((2, 2, t, t), F32), pltpu.VMEM((2, t, t), BF16)],
        compiler_params=_params("parallel", "parallel", "arbitrary"),
        name="df_attn",
    )(qn, kn, vb, b0, b1, far, lam, sg)


def _ml_attn(qh, kh, vb, mask, batch, seq):
    m = qh.shape[0]
    t = ATT_TILE
    nq = seq // t
    return pl.pallas_call(
        functools.partial(_attn_kernel, n_maps=1, has_bias=False, finish=_ml_finish),
        grid=(batch, ML_HEADS, nq),
        in_specs=[pl.BlockSpec((t, ML_HEAD_PAD), lambda b, h, i: (b * nq + i, h)),
                  pl.BlockSpec((seq, ML_HEAD_PAD), lambda b, h, i: (b, h)),
                  pl.BlockSpec((seq, ML_DV), lambda b, h, i: (b, h)),
                  pl.BlockSpec((1, t, t), lambda b, h, i: (0, 0, 0))],
        out_specs=pl.BlockSpec((t, ML_DV), lambda b, h, i: (b * nq + i, h)),
        out_shape=jax.ShapeDtypeStruct((m, ML_W), BF16),
        scratch_shapes=[pltpu.VMEM((1, t, LANES), F32), pltpu.VMEM((1, t, LANES), F32),
                        pltpu.VMEM((1, t, ML_DV), F32), pltpu.VMEM((2, 1, t, t), F32), pltpu.VMEM((1, t, t), BF16)],
        compiler_params=_params("parallel", "parallel", "arbitrary"),
        name="ml_attn",
    )(qh, kh, vb, mask)


def _rope_pad(x, cos, sin):
    return x * cos + pltpu.roll(x, LANES // 2, 1) * sin


def _ml_prep_kernel(cq_ref, ckv_ref, kr_ref, qln_ref, kvln_ref, wqn_ref, wqr_ref, wkn_ref, wv_ref,
                    qgn_ref, qgr_ref, kgn_ref, kgr_ref, cos_ref, sin_ref, qo_ref, ko_ref, vo_ref):
    cos = cos_ref[...]
    sin = sin_ref[...]
    cq = _rms_rows(cq_ref[...], qln_ref[...]).astype(BF16)
    ckv = _rms_rows(ckv_ref[...], kvln_ref[...]).astype(BF16)
    q_nope = _dot(cq, wqn_ref[...])
    q_rope = _dot(cq, wqr_ref[...])
    k_nope = _dot(ckv, wkn_ref[...])
    vo_ref[...] = _dot(ckv, wv_ref[...]).astype(BF16)
    k_raw = kr_ref[...]
    lane = lax.broadcasted_iota(jnp.int32, k_raw.shape, 1)
    k_lo = jnp.where(lane < ML_ROPE // 2, k_raw, 0.0)
    k_rope = k_lo + pltpu.roll(k_raw - k_lo, ML_ROPE // 2, 1)
    kr_sq = k_rope * k_rope
    kr_rot = _rope_pad(k_rope * kgr_ref[...], cos, sin)
    qgn, qgr, kgn = qgn_ref[...], qgr_ref[...], kgn_ref[...]
    for h in range(ML_HEADS):
        sl = slice(h * LANES, (h + 1) * LANES)
        qn, qr, kn = q_nope[:, sl], q_rope[:, sl], k_nope[:, sl]
        q_ss = jnp.sum(qn * qn + qr * qr, axis=-1, keepdims=True)
        q_rs = lax.rsqrt(q_ss * (1.0 / ML_DQK) + EPS) * (ML_DQK ** -0.5 * LOG2E)
        k_ss = jnp.sum(kn * kn + kr_sq, axis=-1, keepdims=True)
        k_rs = lax.rsqrt(k_ss * (1.0 / ML_DQK) + EPS)
        base = h * ML_HEAD_PAD
        qo_ref[:, base:base + LANES] = (qn * q_rs * qgn).astype(BF16)
        qo_ref[:, base + LANES:base + 2 * LANES] = (_rope_pad(qr * qgr, cos, sin) * q_rs).astype(BF16)
        ko_ref[:, base:base + LANES] = (kn * k_rs * kgn).astype(BF16)
        ko_ref[:, base + LANES:base + 2 * LANES] = (kr_rot * k_rs).astype(BF16)


def _ml_prep(proj, qln, kvln, wqn, wqr, wkn, wv, qgn, qgr, kgn, kgr, cos, sin, seq, layer):
    m = proj.shape[0]
    npos = seq // ROW_TILE

    def full(a):
        nd = a.ndim
        if nd == 3:
            return pl.BlockSpec((None,) + a.shape[1:], lambda i: (layer, 0, 0))
        return pl.BlockSpec(a.shape, lambda i: (0, 0))

    pos = pl.BlockSpec((ROW_TILE, LANES), lambda i: (i % npos, 0))
    hp = ML_HEADS * ML_HEAD_PAD
    return pl.pallas_call(
        _ml_prep_kernel,
        grid=(m // ROW_TILE,),
        in_specs=[pl.BlockSpec((ROW_TILE, ML_Q_RANK), lambda i: (i, _MLQ_COL)),
                  pl.BlockSpec((ROW_TILE, ML_KV_RANK), lambda i: (i, _MLKV_COL)),
                  pl.BlockSpec((ROW_TILE, LANES), lambda i: (i, _MLR_COL)),
                  full(qln), full(kvln), full(wqn), full(wqr), full(wkn), full(wv),
                  full(qgn), full(qgr), full(kgn), full(kgr), pos, pos],
        out_specs=[pl.BlockSpec((ROW_TILE, hp), lambda i: (i, 0)),
                   pl.BlockSpec((ROW_TILE, hp), lambda i: (i, 0)),
                   pl.BlockSpec((ROW_TILE, ML_W), lambda i: (i, 0))],
        out_shape=[jax.ShapeDtypeStruct((m, hp), BF16), jax.ShapeDtypeStruct((m, hp), BF16),
                   jax.ShapeDtypeStruct((m, ML_W), BF16)],
        compiler_params=_params("parallel"),
        name="ml_prep",
    )(proj, proj, proj, qln, kvln, wqn, wqr, wkn, wv, qgn, qgr, kgn, kgr, cos, sin)


def _out_kernel(x_ref, hg_ref, df_ref, ml_ref, w1_ref, w2_ref, w3_ref, o_ref):
    o_ref[...] = (x_ref[...] + _dot(hg_ref[...], w1_ref[...]) + _dot(df_ref[...], w2_ref[...])
                  + _dot(ml_ref[...], w3_ref[...]))


def _out_proj(x, o_hg, o_df, o_ml, w1, w2, w3, layer):
    m, d = x.shape

    def rows(w):
        return pl.BlockSpec((ROW_TILE, w), lambda i: (i, 0))

    def wspec(w):
        return pl.BlockSpec((None,) + w.shape[1:], lambda i: (layer, 0, 0))

    return pl.pallas_call(
        _out_kernel,
        grid=(m // ROW_TILE,),
        in_specs=[rows(d), rows(HG_W), rows(DF_W), rows(ML_W), wspec(w1), wspec(w2), wspec(w3)],
        out_specs=rows(d),
        out_shape=jax.ShapeDtypeStruct((m, d), F32),
        compiler_params=_params("parallel"),
        name="out_proj",
    )(x, o_hg, o_df, o_ml, w1, w2, w3)


def _t5_bucket(rel):
    half = REL_BUCKETS // 2
    max_exact = half // 2
    ret = (rel > 0).astype(jnp.int32) * half
    n = jnp.abs(rel)
    large = max_exact + (jnp.log(jnp.maximum(n, 1).astype(F32) / max_exact)
                         / math.log(REL_MAX_DIST / max_exact) * (half - max_exact)).astype(jnp.int32)
    large = jnp.minimum(large, half - 1)
    return ret + jnp.where(n < max_exact, n, large)


def _chunk_mask_tile(t):
    ii = jnp.arange(t)
    return (ii[None, :] // CHUNK) <= (ii[:, None] // CHUNK)


def _bias_tiles(rel_bias, t):
    assert t >= REL_MAX_DIST and t % CHUNK == 0
    table = rel_bias.astype(F32)

    def lookup(bucket):
        out = jnp.zeros((DF_HEADS,) + bucket.shape, F32)
        for b in range(REL_BUCKETS):
            out = out + jnp.where(bucket[None] == b, table[b][:, None, None], 0.0)
        return out

    ii = jnp.arange(t)
    rel = ii[None, :] - ii[:, None]
    b0 = jnp.where(_chunk_mask_tile(t)[None], lookup(_t5_bucket(rel)) * LOG2E, NEG_BIG)
    b1 = lookup(_t5_bucket(rel - t)) * LOG2E
    far = table[_t5_bucket(jnp.full((), -2 * t, jnp.int32))] * LOG2E
    far = jnp.broadcast_to(far[:, None, None], (DF_HEADS, 8, LANES))
    return b0, b1, far


def _rope_tables(seq):
    r = ML_ROPE
    freqs = ROPE_BASE ** (-jnp.arange(0, r, 2, dtype=F32) / r)
    ang = jnp.arange(seq).astype(F32)[:, None] * freqs[None, :]
    cos, sin = jnp.cos(ang), jnp.sin(ang)
    return (_spread_rope(jnp.concatenate([cos, cos], axis=-1)),
            _spread_rope(jnp.concatenate([-sin, sin], axis=-1)))


def _spread_rope(a):
    half = ML_ROPE // 2
    z = jnp.zeros(a.shape[:-1] + (LANES // 2 - half,), a.dtype)
    return jnp.concatenate([a[..., :half], z, a[..., half:], z], axis=-1)


def _pad_axis(a, n, axis):
    shape = list(a.shape)
    shape[axis] = n - a.shape[axis]
    return jnp.concatenate([a, jnp.zeros(shape, a.dtype)], axis=axis)


def _pad_last(a, n):
    return _pad_axis(a, n, a.ndim - 1)


def _spread_rope_heads(a):
    lead = a.shape[:-1]
    return _spread_rope(a.reshape(lead + (ML_HEADS, ML_ROPE))).reshape(lead + (ML_HEADS * LANES,))


def kernel(x, ffn_a_norm, ffn_a_w_gate, ffn_a_w_up, ffn_a_w_down, mix_norm, w_in, w_out, hgrn_lb_logits, hgrn_out_norm, diff_q_norm, diff_k_norm, diff_lambda_q1, diff_lambda_k1, diff_lambda_q2, diff_lambda_k2, diff_subln, rel_bias, mla_q_lora_norm, mla_w_uq, mla_kv_lora_norm, mla_w_ukv, mla_q_norm, mla_k_norm, ffn_b_norm, ffn_b_w_gate, ffn_b_w_up, ffn_b_w_down):
    batch, seq, d = x.shape
    depth = w_in.shape[0]
    m = batch * seq
    assert m % PROJ_ROWS == 0 and m % FFN_ROWS == 0
    assert seq % ROW_TILE == 0 and seq % ATT_TILE == 0 and seq % HG_ROWS == 0

    def ffn_weights(wg, wu, wd):
        return (_cast_pad_cols(wg, D_FF_PAD), _cast_pad_cols(wu, D_FF_PAD),
                _pad_axis(wd.astype(BF16), D_FF_PAD, 1))

    wa = ffn_weights(ffn_a_w_gate, ffn_a_w_up, ffn_a_w_down)
    wb = ffn_weights(ffn_b_w_gate, ffn_b_w_up, ffn_b_w_down)
    w_in_p = _cast_pad_cols(w_in, P_IN_PAD)
    w_o = w_out.astype(BF16)
    w_o1, w_o2, w_o3 = w_o[:, :HG_W], w_o[:, HG_W:HG_W + DF_W], w_o[:, HG_W + DF_W:]

    uq = mla_w_uq.reshape(depth, ML_Q_RANK, ML_HEADS, ML_DQK)
    wqn = uq[..., :ML_NOPE].reshape(depth, ML_Q_RANK, ML_HEADS * ML_NOPE).astype(BF16)
    wqr = _spread_rope_heads(uq[..., ML_NOPE:].reshape(depth, ML_Q_RANK, ML_HEADS * ML_ROPE)).astype(BF16)
    ukv = mla_w_ukv.reshape(depth, ML_KV_RANK, ML_HEADS, ML_NOPE + ML_DV)
    wkn = ukv[..., :ML_NOPE].reshape(depth, ML_KV_RANK, ML_HEADS * ML_NOPE).astype(BF16)
    wv = ukv[..., ML_NOPE:].reshape(depth, ML_KV_RANK, ML_HEADS * ML_DV).astype(BF16)

    lb_all = jnp.cumsum(jax.nn.softmax(hgrn_lb_logits.astype(F32), axis=0), axis=0)
    lb_all = lb_all - lb_all[0:1]

    hg_masks = jnp.asarray(_hgrn_level_masks())
    hg_row_masks = jnp.asarray(_hgrn_row_masks())
    b0, b1, far = _bias_tiles(rel_bias, ATT_TILE)
    ml_mask = jnp.where(_chunk_mask_tile(ATT_TILE), 0.0, NEG_BIG).astype(F32)[None]
    cos, sin = _rope_tables(seq)
    ones_blk = jnp.asarray(np.kron(np.eye(DF_W // DF_DQK), np.ones((DF_DQK, DF_DQK))), BF16)

    def row3(a):
        return a.reshape(a.shape[0], 1, a.shape[1])

    g_a, g_mix, g_b = row3(ffn_a_norm), row3(mix_norm), row3(ffn_b_norm)
    g_qln, g_kvln = row3(mla_q_lora_norm), row3(mla_kv_lora_norm)

    xf = x.reshape(m, d)
    for l in range(depth):
        xf = _ffn(xf, g_a, *wa, l)

        proj = _proj(xf, g_mix, w_in_p, l)
        o_hg = _hgrn(proj, lb_all[l][None, :], hgrn_out_norm[l][None, :], hg_masks, hg_row_masks,
                     batch, seq)

        lambda_init = 0.8 - 0.6 * math.exp(-0.3 * l)
        lam = (jnp.exp(jnp.sum(diff_lambda_q1[l].astype(F32) * diff_lambda_k1[l].astype(F32)))
               - jnp.exp(jnp.sum(diff_lambda_q2[l].astype(F32) * diff_lambda_k2[l].astype(F32)))
               + lambda_init)
        qn, kn, vb = _df_prep(proj, jnp.tile(diff_q_norm[l], DF_W // DF_DQK)[None, :],
                              jnp.tile(diff_k_norm[l], DF_W // DF_DQK)[None, :], ones_blk)
        o_df = _df_attn(qn, kn, vb, b0, b1, far, jnp.full((1, DF_DV), lam, F32),
                        diff_subln[l][None, :], batch, seq, lambda_init)

        qg, kg = mla_q_norm[l], mla_k_norm[l]
        qh, kh, vm = _ml_prep(proj, g_qln, g_kvln, wqn, wqr, wkn, wv,
                              qg[None, :ML_NOPE], _spread_rope(qg[None, ML_NOPE:]),
                              kg[None, :ML_NOPE], _spread_rope(kg[None, ML_NOPE:]),
                              cos, sin, seq, l)
        o_ml = _ml_attn(qh, kh, vm, ml_mask, batch, seq)

        xf = _out_proj(xf, o_hg, o_df, o_ml, w_o1, w_o2, w_o3, l)
        xf = _ffn(xf, g_b, *wb, l)
    return xf.reshape(batch, seq, d)
```

```python
import functools
import math

import jax
import jax.numpy as jnp
import numpy as np
from jax import lax
from jax.experimental import pallas as pl
from jax.experimental.pallas import tpu as pltpu

F32 = jnp.float32
BF16 = jnp.bfloat16

D_MODEL = 2048
DEPTH = 4
CHUNK = 64
D_FF = 5504
EPS = 1e-6
HG_HEADS = 6
HG_DK = 128
HG_DV = 128
HG_W = HG_HEADS * HG_DK
DF_HEADS = 4
DF_DQK = 64
DF_DV = 128
DF_W = DF_HEADS * DF_DV
ML_HEADS = 6
ML_Q_RANK = 512
ML_KV_RANK = 256
ML_NOPE = 128
ML_ROPE = 64
ML_DV = 128
ML_DQK = ML_NOPE + ML_ROPE
ML_W = ML_HEADS * ML_DV
ROPE_BASE = 10000.0
REL_BUCKETS = 32
REL_MAX_DIST = 128
P_IN = 4 * HG_W + 3 * DF_W + ML_Q_RANK + ML_KV_RANK + ML_ROPE

LANES = 128
ML_HEAD_PAD = 2 * LANES
FF_TILE = 512
FF_SUB = 256
D_FF_PAD = -(-D_FF // FF_TILE) * FF_TILE
P_IN_PAD = -(-P_IN // FF_TILE) * FF_TILE
ROW_TILE = 1024
CAST_ROWS = 256
HG_ROWS = 512
ATT_TILE = 512
ATT_ROWS = 256
LOG2E = math.log2(math.e)
NEG_BIG = -1e30
VMEM_LIMIT = 56 * 1024 * 1024

_HG_COL = 0
_DF_COL = (4 * HG_W) // DF_W
_MLQ_COL = (4 * HG_W + 3 * DF_W) // ML_Q_RANK
_MLKV_COL = (4 * HG_W + 3 * DF_W + ML_Q_RANK) // ML_KV_RANK
_MLR_COL = (4 * HG_W + 3 * DF_W + ML_Q_RANK + ML_KV_RANK) // LANES


def _dot(a, b):
    return jnp.dot(a, b, preferred_element_type=F32)


def _dot_nt(a, b):
    return lax.dot_general(a, b, (((1,), (1,)), ((), ())), preferred_element_type=F32)


def _dot_tn(a, b):
    return lax.dot_general(a, b, (((0,), (0,)), ((), ())), preferred_element_type=F32)


def _params(*sem):
    return pltpu.CompilerParams(dimension_semantics=sem, vmem_limit_bytes=VMEM_LIMIT)


def _rms_rows(x, gain):
    return x * lax.rsqrt(jnp.mean(x * x, axis=-1, keepdims=True) + EPS) * gain


def _cast_pad_kernel(w_ref, o_ref):
    f = w_ref.shape[-1]
    o_ref[:, :f] = w_ref[...].astype(BF16)
    o_ref[:, f:] = jnp.zeros((o_ref.shape[0], o_ref.shape[1] - f), BF16)


def _cast_pad_cols(w, n):
    depth, d, f = w.shape
    assert f % LANES == 0 and n % LANES == 0 and n > f and d % CAST_ROWS == 0
    return pl.pallas_call(
        _cast_pad_kernel,
        grid=(depth, d // CAST_ROWS),
        in_specs=[pl.BlockSpec((None, CAST_ROWS, f), lambda l, i: (l, i, 0))],
        out_specs=pl.BlockSpec((None, CAST_ROWS, n), lambda l, i: (l, i, 0)),
        out_shape=jax.ShapeDtypeStruct((depth, d, n), BF16),
        compiler_params=_params("parallel", "parallel"),
        name="cast_pad",
    )(w)


def _ffn_kernel(x_ref, g_ref, wg_ref, wu_ref, wd_ref, o_ref, h_ref):
    @pl.when(pl.program_id(1) == 0)
    def _():
        x = x_ref[...]
        h_ref[...] = _rms_rows(x, g_ref[...]).astype(BF16)
        o_ref[...] = x

    h = h_ref[...]
    acts = []
    for c in range(0, FF_TILE, FF_SUB):
        a = _dot(h, wg_ref[:, c:c + FF_SUB])
        u = _dot(h, wu_ref[:, c:c + FF_SUB])
        acts.append(((0.5 * a) * jax.nn.sigmoid(a) * u).astype(BF16))
    o_ref[...] += _dot(jnp.concatenate(acts, axis=-1), wd_ref[...])


def _ffn(x, gains, wg, wu, wd, layer):
    m, d = x.shape
    return pl.pallas_call(
        _ffn_kernel,
        grid=(m // ROW_TILE, wg.shape[-1] // FF_TILE),
        in_specs=[
            pl.BlockSpec((ROW_TILE, d), lambda i, j: (i, 0)),
            pl.BlockSpec((None, 1, d), lambda i, j: (layer, 0, 0)),
            pl.BlockSpec((None, d, FF_TILE), lambda i, j: (layer, 0, j)),
            pl.BlockSpec((None, d, FF_TILE), lambda i, j: (layer, 0, j)),
            pl.BlockSpec((None, FF_TILE, d), lambda i, j: (layer, j, 0)),
        ],
        out_specs=pl.BlockSpec((ROW_TILE, d), lambda i, j: (i, 0)),
        out_shape=jax.ShapeDtypeStruct((m, d), F32),
        scratch_shapes=[pltpu.VMEM((ROW_TILE, d), BF16)],
        compiler_params=_params("parallel", "arbitrary"),
        name="ffn",
    )(x, gains, wg, wu, wd)


def _proj_kernel(x_ref, g_ref, w_ref, o_ref, h_ref):
    @pl.when(pl.program_id(1) == 0)
    def _():
        h_ref[...] = _rms_rows(x_ref[...], g_ref[...]).astype(BF16)

    o_ref[...] = _dot(h_ref[...], w_ref[...])


def _proj(x, gains, w, layer):
    m, d = x.shape
    n = w.shape[-1]
    return pl.pallas_call(
        _proj_kernel,
        grid=(m // ROW_TILE, n // FF_TILE),
        in_specs=[
            pl.BlockSpec((ROW_TILE, d), lambda i, j: (i, 0)),
            pl.BlockSpec((None, 1, d), lambda i, j: (layer, 0, 0)),
            pl.BlockSpec((None, d, FF_TILE), lambda i, j: (layer, 0, j)),
        ],
        out_specs=pl.BlockSpec((ROW_TILE, FF_TILE), lambda i, j: (i, j)),
        out_shape=jax.ShapeDtypeStruct((m, n), F32),
        scratch_shapes=[pltpu.VMEM((ROW_TILE, d), BF16)],
        compiler_params=_params("parallel", "arbitrary"),
        name="proj",
    )(x, gains, w)


def _hgrn_level_masks():
    t = np.arange(CHUNK)[:, None]
    s = np.arange(CHUNK)[None, :]
    levels = [(((t >> l) ^ (s >> l)) == 1) & (t > s) for l in range(6)]
    levels.append(t == s)
    return np.stack(levels).astype(np.float32)


def _hgrn_row_masks():
    t = np.arange(CHUNK)
    pats = [t % 2 == 1, t % 4 == 0, t % 4 != 0, t % 4 == 2, t % 4 == 3]
    pats += [(t >> l) & 1 == 1 for l in range(2, 6)]
    return np.broadcast_to(np.stack(pats)[:, :, None], (len(pats), CHUNK, LANES)).astype(np.float32)


_RM_ODD, _RM_K1, _RM_NK1, _RM_M2, _RM_M3, _RM_UPPER2 = 0, 1, 2, 3, 4, 5


def _block_bcast(p, level):
    n, w = p.shape
    half = 1 << level
    size = 2 * half
    assert size >= 8
    pieces = []
    for b in range(n // size):
        mrow = b * size + half - 1
        pieces.append(jnp.broadcast_to(p[mrow:mrow + 1, :], (size, w)))
    return jnp.concatenate(pieces, axis=0)


def _hgrn_kernel(q_ref, f_ref, i_ref, g_ref, lb_ref, og_ref, mask_ref, rm_ref, o_ref, st_ref):
    @pl.when(pl.program_id(1) == 0)
    def _():
        st_ref[...] = jnp.zeros_like(st_ref)

    og = og_ref[...]

    def head_chunk(rows, h):
        sl = slice(h * HG_DK, (h + 1) * HG_DK)
        lb = lb_ref[:, sl]
        q = q_ref[rows, sl] * (HG_DK ** -0.5)
        f = lb + (1.0 - lb) * jax.nn.sigmoid(f_ref[rows, sl])
        kk = 1.0 - f
        kb = kk.astype(BF16)
        v = i_ref[rows, sl].astype(BF16)
        gate = g_ref[rows, sl]
        p = jnp.log(f) * LOG2E
        q_dec = [f, None]
        k_dec = [None, rm_ref[_RM_K1] * pltpu.roll(f, CHUNK - 1, 0) + rm_ref[_RM_NK1]]
        p = p + rm_ref[_RM_ODD] * pltpu.roll(p, 1, 0)
        q_dec[1] = jnp.exp2(p)
        p = p + rm_ref[_RM_M2] * pltpu.roll(p, 1, 0) + rm_ref[_RM_M3] * pltpu.roll(p, 2, 0)
        for l in range(2, 6):
            tb = _block_bcast(p, l)
            q_dec.append(jnp.exp2(p))
            k_dec.append(jnp.exp2(jnp.minimum(tb - p, 0.0)))
            p = p + rm_ref[_RM_UPPER2 + l - 2] * tb
        last = p[CHUNK - 1:CHUNK, :]

        scores = mask_ref[6] * _dot_nt(q.astype(BF16), kb)
        for l in range(6):
            kl = kb if k_dec[l] is None else (kk * k_dec[l]).astype(BF16)
            scores += mask_ref[l] * _dot_nt((q * q_dec[l]).astype(BF16), kl)
        st = st_ref[h]
        o = _dot_nt((q * jnp.exp2(p)).astype(BF16), st.astype(BF16)) + _dot(scores.astype(BF16), v)
        k_out = (kk * jnp.exp2(last - p)).astype(BF16)
        st_ref[h] = st * jnp.exp2(last) + _dot_tn(v, k_out)
        o = _rms_rows(o, og) * (gate * jax.nn.sigmoid(gate))
        o_ref[rows, sl] = o.astype(o_ref.dtype)

    for c in range(q_ref.shape[0] // CHUNK):
        for h in range(HG_HEADS):
            head_chunk(slice(c * CHUNK, (c + 1) * CHUNK), h)


def _hgrn(proj, lb, og, masks, row_masks, batch, seq):
    m = proj.shape[0]
    ns = seq // HG_ROWS

    def col(c):
        return pl.BlockSpec((HG_ROWS, HG_W), lambda b, s: (b * ns + s, _HG_COL + c))

    return pl.pallas_call(
        _hgrn_kernel,
        grid=(batch, ns),
        in_specs=[col(0), col(1), col(2), col(3),
                  pl.BlockSpec((1, HG_W), lambda b, s: (0, 0)),
                  pl.BlockSpec((1, HG_DV), lambda b, s: (0, 0)),
                  pl.BlockSpec((7, CHUNK, CHUNK), lambda b, s: (0, 0, 0)),
                  pl.BlockSpec(row_masks.shape, lambda b, s: (0, 0, 0))],
        out_specs=pl.BlockSpec((HG_ROWS, HG_W), lambda b, s: (b * ns + s, 0)),
        out_shape=jax.ShapeDtypeStruct((m, HG_W), BF16),
        scratch_shapes=[pltpu.VMEM((HG_HEADS, HG_DV, HG_DK), F32)],
        compiler_params=_params("parallel", "arbitrary"),
        name="hgrn",
    )(proj, proj, proj, proj, lb, og, masks, row_masks)


def _group_sumsq(x, ones_blk):
    x2 = x * x
    hi = x2.astype(BF16)
    lo = (x2 - hi.astype(F32)).astype(BF16)
    return _dot(hi, ones_blk) + _dot(lo, ones_blk)


def _df_prep_kernel(q_ref, k_ref, v_ref, qg_ref, kg_ref, ones_ref, qo_ref, ko_ref, vo_ref):
    ones_blk = ones_ref[...]
    q = q_ref[...]
    k = k_ref[...]
    qn = q * lax.rsqrt(_group_sumsq(q, ones_blk) * (1.0 / DF_DQK) + EPS) * qg_ref[...]
    kn = k * lax.rsqrt(_group_sumsq(k, ones_blk) * (1.0 / DF_DQK) + EPS) * kg_ref[...]
    qo_ref[...] = (qn * (DF_DQK ** -0.5 * LOG2E)).astype(BF16)
    ko_ref[...] = kn.astype(BF16)
    vo_ref[...] = v_ref[...].astype(BF16)


def _df_prep(proj, qg, kg, ones_blk):
    m = proj.shape[0]

    def col(c):
        return pl.BlockSpec((ROW_TILE, DF_W), lambda i: (i, _DF_COL + c))

    row = pl.BlockSpec((ROW_TILE, DF_W), lambda i: (i, 0))
    vec = pl.BlockSpec((1, DF_W), lambda i: (0, 0))
    out = jax.ShapeDtypeStruct((m, DF_W), BF16)
    return pl.pallas_call(
        _df_prep_kernel,
        grid=(m // ROW_TILE,),
        in_specs=[col(0), col(1), col(2), vec, vec, pl.BlockSpec((DF_W, DF_W), lambda i: (0, 0))],
        out_specs=[row, row, row],
        out_shape=[out, out, out],
        compiler_params=_params("parallel"),
        name="df_prep",
    )(proj, proj, proj, qg, kg, ones_blk)


def _attn_kernel(*refs, n_maps, has_bias, finish):
    if has_bias:
        q_ref, k_ref, v_ref, b0_ref, b1_ref, far_ref, *rest = refs
    else:
        q_ref, k_ref, v_ref, b0_ref, *rest = refs
        b1_ref = far_ref = None
    *extra, o_ref, m_ref, l_ref, acc_ref, s_ref = rest
    i = pl.program_id(2)
    t = q_ref.shape[0]
    nc = t // LANES
    if n_maps == 2:
        q = q_ref[...]
        lane = lax.broadcasted_iota(jnp.int32, q.shape, 1)
        zero = jnp.zeros_like(q)
        qs = [jnp.where(lane < DF_DQK, q, zero), jnp.where(lane >= DF_DQK, q, zero)]
    else:
        qs = [q_ref[...]]
    q_all = qs[0] if n_maps == 1 else jnp.concatenate(qs, axis=0)

    m_ref[...] = jnp.full_like(m_ref, NEG_BIG)
    l_ref[...] = jnp.zeros_like(l_ref)
    acc_ref[...] = jnp.zeros_like(acc_ref)

    chains = [(mi, slice(r0, r0 + ATT_ROWS)) for mi in range(n_maps) for r0 in range(0, t, ATT_ROWS)]

    def key_rows(n):
        return pl.ds(pl.multiple_of(n * t, t), t)

    def visible(r, diag):
        return r.stop // LANES if diag else nc

    def logits(n, slot, diag=False):
        kt = k_ref[key_rows(n), :]
        if not diag:
            s = _dot_nt(q_all, kt)
            for mi in range(n_maps):
                s_ref[slot, mi] = s[mi * t:(mi + 1) * t]
            return
        for mi, r in chains:
            w = visible(r, diag) * LANES
            s_ref[slot, mi, r, :w] = _dot_nt(qs[mi][r], kt[:w])

    def consume(n, slot, bias_ref, const, diag=False):
        vt = v_ref[key_rows(n), :]
        for mi, r in chains:
            def col(c):
                x = s_ref[slot, mi, r, c * LANES:(c + 1) * LANES]
                return x if bias_ref is None else x + bias_ref[0, r, c * LANES:(c + 1) * LANES]

            n_cols = visible(r, diag)
            m_prev = m_ref[mi, r]
            tile_max = functools.reduce(jnp.maximum, [col(c) for c in range(n_cols)])
            row_max = jnp.max(tile_max, axis=-1, keepdims=True)
            if const is None:
                m_new = jnp.maximum(m_prev, row_max)
                shift = m_new
            else:
                m_new = jnp.maximum(m_prev, row_max + const)
                shift = m_new - const
            m_ref[mi, r] = m_new
            alpha = jnp.exp2(m_prev - m_new)
            ps = [jnp.exp2(col(c) - shift) for c in range(n_cols)]
            l_ref[mi, r] = alpha * l_ref[mi, r] + functools.reduce(jnp.add, ps)
            p = jnp.concatenate([x.astype(BF16) for x in ps], axis=-1)
            acc_ref[mi, r] = alpha * acc_ref[mi, r] + _dot(p, vt[:p.shape[1]])

    far = far_ref[0, 0:1, 0:1] if has_bias else None
    prev_bias = b1_ref if has_bias else None

    n_far = jnp.maximum(i - 1, 0)
    n_pairs = lax.shift_right_logical(n_far, 1)
    logits(0, 0)

    def pair_body(j, carry):
        n = 2 * j
        logits(n + 1, 1)
        consume(n, 0, None, far)
        logits(n + 2, 0)
        consume(n + 1, 1, None, far)
        return carry

    lax.fori_loop(0, n_pairs, pair_body, 0)

    @pl.when(i == 0)
    def _():
        consume(0, 0, b0_ref, None, diag=True)

    @pl.when((i >= 1) & ((n_far & 1) == 0))
    def _():
        logits(i, 1, diag=True)
        consume(i - 1, 0, prev_bias, None)
        consume(i, 1, b0_ref, None, diag=True)

    @pl.when((n_far & 1) == 1)
    def _():
        logits(i - 1, 1)
        consume(i - 2, 0, None, far)
        logits(i, 0, diag=True)
        consume(i - 1, 1, prev_bias, None)
        consume(i, 0, b0_ref, None, diag=True)

    outs = [acc_ref[mi] / jnp.sum(l_ref[mi], axis=-1, keepdims=True) for mi in range(n_maps)]
    o_ref[...] = finish(outs, *extra).astype(o_ref.dtype)


def _df_finish(outs, lam_ref, sg_ref, *, lambda_init):
    o = outs[0] - lam_ref[...] * outs[1]
    return _rms_rows(o, sg_ref[...]) * (1.0 - lambda_init)


def _ml_finish(outs):
    return outs[0]


def _df_attn(qn, kn, vb, b0, b1, far, lam, sg, batch, seq, lambda_init):
    m = qn.shape[0]
    t = ATT_TILE
    nq = seq // t
    kv = pl.BlockSpec((seq, DF_DV), lambda b, h, i: (b, h))
    tile = pl.BlockSpec((1, t, t), lambda b, h, i: (h, 0, 0))
    vec = pl.BlockSpec((1, DF_DV), lambda b, h, i: (0, 0))
    kern = functools.partial(_attn_kernel, n_maps=2, has_bias=True,
                             finish=functools.partial(_df_finish, lambda_init=lambda_init))
    return pl.pallas_call(
        kern,
        grid=(batch, DF_HEADS, nq),
        in_specs=[pl.BlockSpec((t, DF_DV), lambda b, h, i: (b * nq + i, h)), kv, kv, tile, tile,
                  pl.BlockSpec((1, 8, LANES), lambda b, h, i: (h, 0, 0)), vec, vec],
        out_specs=pl.BlockSpec((t, DF_DV), lambda b, h, i: (b * nq + i, h)),
        out_shape=jax.ShapeDtypeStruct((m, DF_W), BF16),
        scratch_shapes=[pltpu.VMEM((2, t, LANES), F32), pltpu.VMEM((2, t, LANES), F32),
                        pltpu.VMEM((2, t, DF_DV), F32), pltpu.VMEM((2, 2, t, t), F32)],
        compiler_params=_params("parallel", "parallel", "arbitrary"),
        name="df_attn",
    )(qn, kn, vb, b0, b1, far, lam, sg)


def _ml_attn(qh, kh, vb, mask, batch, seq):
    m = qh.shape[0]
    t = ATT_TILE
    nq = seq // t
    return pl.pallas_call(
        functools.partial(_attn_kernel, n_maps=1, has_bias=False, finish=_ml_finish),
        grid=(batch, ML_HEADS, nq),
        in_specs=[pl.BlockSpec((t, ML_HEAD_PAD), lambda b, h, i: (b * nq + i, h)),
                  pl.BlockSpec((seq, ML_HEAD_PAD), lambda b, h, i: (b, h)),
                  pl.BlockSpec((seq, ML_DV), lambda b, h, i: (b, h)),
                  pl.BlockSpec((1, t, t), lambda b, h, i: (0, 0, 0))],
        out_specs=pl.BlockSpec((t, ML_DV), lambda b, h, i: (b * nq + i, h)),
        out_shape=jax.ShapeDtypeStruct((m, ML_W), BF16),
        scratch_shapes=[pltpu.VMEM((1, t, LANES), F32), pltpu.VMEM((1, t, LANES), F32),
                        pltpu.VMEM((1, t, ML_DV), F32), pltpu.VMEM((2, 1, t, t), F32)],
        compiler_params=_params("parallel", "parallel", "arbitrary"),
        name="ml_attn",
    )(qh, kh, vb, mask)


def _rope_pad(x, cos, sin):
    return x * cos + pltpu.roll(x, LANES // 2, 1) * sin


def _ml_prep_kernel(cq_ref, ckv_ref, kr_ref, qln_ref, kvln_ref, wqn_ref, wqr_ref, wkn_ref, wv_ref,
                    qgn_ref, qgr_ref, kgn_ref, kgr_ref, cos_ref, sin_ref, qo_ref, ko_ref, vo_ref):
    cos = cos_ref[...]
    sin = sin_ref[...]
    cq = _rms_rows(cq_ref[...], qln_ref[...]).astype(BF16)
    ckv = _rms_rows(ckv_ref[...], kvln_ref[...]).astype(BF16)
    q_nope = _dot(cq, wqn_ref[...])
    q_rope = _dot(cq, wqr_ref[...])
    k_nope = _dot(ckv, wkn_ref[...])
    vo_ref[...] = _dot(ckv, wv_ref[...]).astype(BF16)
    k_raw = kr_ref[...]
    lane = lax.broadcasted_iota(jnp.int32, k_raw.shape, 1)
    k_lo = jnp.where(lane < ML_ROPE // 2, k_raw, 0.0)
    k_rope = k_lo + pltpu.roll(k_raw - k_lo, ML_ROPE // 2, 1)
    kr_sq = k_rope * k_rope
    kr_rot = _rope_pad(k_rope * kgr_ref[...], cos, sin)
    qgn, qgr, kgn = qgn_ref[...], qgr_ref[...], kgn_ref[...]
    for h in range(ML_HEADS):
        sl = slice(h * LANES, (h + 1) * LANES)
        qn, qr, kn = q_nope[:, sl], q_rope[:, sl], k_nope[:, sl]
        q_ss = jnp.sum(qn * qn + qr * qr, axis=-1, keepdims=True)
        q_rs = lax.rsqrt(q_ss * (1.0 / ML_DQK) + EPS) * (ML_DQK ** -0.5 * LOG2E)
        k_ss = jnp.sum(kn * kn + kr_sq, axis=-1, keepdims=True)
        k_rs = lax.rsqrt(k_ss * (1.0 / ML_DQK) + EPS)
        base = h * ML_HEAD_PAD
        qo_ref[:, base:base + LANES] = (qn * q_rs * qgn).astype(BF16)
        qo_ref[:, base + LANES:base + 2 * LANES] = (_rope_pad(qr * qgr, cos, sin) * q_rs).astype(BF16)
        ko_ref[:, base:base + LANES] = (kn * k_rs * kgn).astype(BF16)
        ko_ref[:, base + LANES:base + 2 * LANES] = (kr_rot * k_rs).astype(BF16)


def _ml_prep(proj, qln, kvln, wqn, wqr, wkn, wv, qgn, qgr, kgn, kgr, cos, sin, seq, layer):
    m = proj.shape[0]
    npos = seq // ROW_TILE

    def full(a):
        nd = a.ndim
        if nd == 3:
            return pl.BlockSpec((None,) + a.shape[1:], lambda i: (layer, 0, 0))
        return pl.BlockSpec(a.shape, lambda i: (0, 0))

    pos = pl.BlockSpec((ROW_TILE, LANES), lambda i: (i % npos, 0))
    hp = ML_HEADS * ML_HEAD_PAD
    return pl.pallas_call(
        _ml_prep_kernel,
        grid=(m // ROW_TILE,),
        in_specs=[pl.BlockSpec((ROW_TILE, ML_Q_RANK), lambda i: (i, _MLQ_COL)),
                  pl.BlockSpec((ROW_TILE, ML_KV_RANK), lambda i: (i, _MLKV_COL)),
                  pl.BlockSpec((ROW_TILE, LANES), lambda i: (i, _MLR_COL)),
                  full(qln), full(kvln), full(wqn), full(wqr), full(wkn), full(wv),
                  full(qgn), full(qgr), full(kgn), full(kgr), pos, pos],
        out_specs=[pl.BlockSpec((ROW_TILE, hp), lambda i: (i, 0)),
                   pl.BlockSpec((ROW_TILE, hp), lambda i: (i, 0)),
                   pl.BlockSpec((ROW_TILE, ML_W), lambda i: (i, 0))],
        out_shape=[jax.ShapeDtypeStruct((m, hp), BF16), jax.ShapeDtypeStruct((m, hp), BF16),
                   jax.ShapeDtypeStruct((m, ML_W), BF16)],
        compiler_params=_params("parallel"),
        name="ml_prep",
    )(proj, proj, proj, qln, kvln, wqn, wqr, wkn, wv, qgn, qgr, kgn, kgr, cos, sin)


def _out_kernel(x_ref, hg_ref, df_ref, ml_ref, w1_ref, w2_ref, w3_ref, o_ref):
    o_ref[...] = (x_ref[...] + _dot(hg_ref[...], w1_ref[...]) + _dot(df_ref[...], w2_ref[...])
                  + _dot(ml_ref[...], w3_ref[...]))


def _out_proj(x, o_hg, o_df, o_ml, w1, w2, w3, layer):
    m, d = x.shape

    def rows(w):
        return pl.BlockSpec((ROW_TILE, w), lambda i: (i, 0))

    def wspec(w):
        return pl.BlockSpec((None,) + w.shape[1:], lambda i: (layer, 0, 0))

    return pl.pallas_call(
        _out_kernel,
        grid=(m // ROW_TILE,),
        in_specs=[rows(d), rows(HG_W), rows(DF_W), rows(ML_W), wspec(w1), wspec(w2), wspec(w3)],
        out_specs=rows(d),
        out_shape=jax.ShapeDtypeStruct((m, d), F32),
        compiler_params=_params("parallel"),
        name="out_proj",
    )(x, o_hg, o_df, o_ml, w1, w2, w3)


def _t5_bucket(rel):
    half = REL_BUCKETS // 2
    max_exact = half // 2
    ret = (rel > 0).astype(jnp.int32) * half
    n = jnp.abs(rel)
    large = max_exact + (jnp.log(jnp.maximum(n, 1).astype(F32) / max_exact)
                         / math.log(REL_MAX_DIST / max_exact) * (half - max_exact)).astype(jnp.int32)
    large = jnp.minimum(large, half - 1)
    return ret + jnp.where(n < max_exact, n, large)


def _chunk_mask_tile(t):
    ii = jnp.arange(t)
    return (ii[None, :] // CHUNK) <= (ii[:, None] // CHUNK)


def _bias_tiles(rel_bias, t):
    assert t >= REL_MAX_DIST and t % CHUNK == 0
    table = rel_bias.astype(F32)

    def lookup(bucket):
        out = jnp.zeros((DF_HEADS,) + bucket.shape, F32)
        for b in range(REL_BUCKETS):
            out = out + jnp.where(bucket[None] == b, table[b][:, None, None], 0.0)
        return out

    ii = jnp.arange(t)
    rel = ii[None, :] - ii[:, None]
    b0 = jnp.where(_chunk_mask_tile(t)[None], lookup(_t5_bucket(rel)) * LOG2E, NEG_BIG)
    b1 = lookup(_t5_bucket(rel - t)) * LOG2E
    far = table[_t5_bucket(jnp.full((), -2 * t, jnp.int32))] * LOG2E
    far = jnp.broadcast_to(far[:, None, None], (DF_HEADS, 8, LANES))
    return b0, b1, far


def _rope_tables(seq):
    r = ML_ROPE
    freqs = ROPE_BASE ** (-jnp.arange(0, r, 2, dtype=F32) / r)
    ang = jnp.arange(seq).astype(F32)[:, None] * freqs[None, :]
    cos, sin = jnp.cos(ang), jnp.sin(ang)
    return (_spread_rope(jnp.concatenate([cos, cos], axis=-1)),
            _spread_rope(jnp.concatenate([-sin, sin], axis=-1)))


def _spread_rope(a):
    half = ML_ROPE // 2
    z = jnp.zeros(a.shape[:-1] + (LANES // 2 - half,), a.dtype)
    return jnp.concatenate([a[..., :half], z, a[..., half:], z], axis=-1)


def _pad_axis(a, n, axis):
    shape = list(a.shape)
    shape[axis] = n - a.shape[axis]
    return jnp.concatenate([a, jnp.zeros(shape, a.dtype)], axis=axis)


def _pad_last(a, n):
    return _pad_axis(a, n, a.ndim - 1)


def _spread_rope_heads(a):
    lead = a.shape[:-1]
    return _spread_rope(a.reshape(lead + (ML_HEADS, ML_ROPE))).reshape(lead + (ML_HEADS * LANES,))


def kernel(x, ffn_a_norm, ffn_a_w_gate, ffn_a_w_up, ffn_a_w_down, mix_norm, w_in, w_out, hgrn_lb_logits, hgrn_out_norm, diff_q_norm, diff_k_norm, diff_lambda_q1, diff_lambda_k1, diff_lambda_q2, diff_lambda_k2, diff_subln, rel_bias, mla_q_lora_norm, mla_w_uq, mla_kv_lora_norm, mla_w_ukv, mla_q_norm, mla_k_norm, ffn_b_norm, ffn_b_w_gate, ffn_b_w_up, ffn_b_w_down):
    batch, seq, d = x.shape
    depth = w_in.shape[0]
    m = batch * seq
    assert seq % ROW_TILE == 0 and seq % ATT_TILE == 0 and seq % HG_ROWS == 0

    def ffn_weights(wg, wu, wd):
        return (_cast_pad_cols(wg, D_FF_PAD), _cast_pad_cols(wu, D_FF_PAD),
                _pad_axis(wd.astype(BF16), D_FF_PAD, 1))

    wa = ffn_weights(ffn_a_w_gate, ffn_a_w_up, ffn_a_w_down)
    wb = ffn_weights(ffn_b_w_gate, ffn_b_w_up, ffn_b_w_down)
    w_in_p = _pad_last(w_in.astype(BF16), P_IN_PAD)
    w_o = w_out.astype(BF16)
    w_o1, w_o2, w_o3 = w_o[:, :HG_W], w_o[:, HG_W:HG_W + DF_W], w_o[:, HG_W + DF_W:]

    uq = mla_w_uq.reshape(depth, ML_Q_RANK, ML_HEADS, ML_DQK)
    wqn = uq[..., :ML_NOPE].reshape(depth, ML_Q_RANK, ML_HEADS * ML_NOPE).astype(BF16)
    wqr = _spread_rope_heads(uq[..., ML_NOPE:].reshape(depth, ML_Q_RANK, ML_HEADS * ML_ROPE)).astype(BF16)
    ukv = mla_w_ukv.reshape(depth, ML_KV_RANK, ML_HEADS, ML_NOPE + ML_DV)
    wkn = ukv[..., :ML_NOPE].reshape(depth, ML_KV_RANK, ML_HEADS * ML_NOPE).astype(BF16)
    wv = ukv[..., ML_NOPE:].reshape(depth, ML_KV_RANK, ML_HEADS * ML_DV).astype(BF16)

    lb_all = jnp.cumsum(jax.nn.softmax(hgrn_lb_logits.astype(F32), axis=0), axis=0)
    lb_all = lb_all - lb_all[0:1]

    hg_masks = jnp.asarray(_hgrn_level_masks())
    hg_row_masks = jnp.asarray(_hgrn_row_masks())
    b0, b1, far = _bias_tiles(rel_bias, ATT_TILE)
    ml_mask = jnp.where(_chunk_mask_tile(ATT_TILE), 0.0, NEG_BIG).astype(F32)[None]
    cos, sin = _rope_tables(seq)
    ones_blk = jnp.asarray(np.kron(np.eye(DF_W // DF_DQK), np.ones((DF_DQK, DF_DQK))), BF16)

    def row3(a):
        return a.reshape(a.shape[0], 1, a.shape[1])

    g_a, g_mix, g_b = row3(ffn_a_norm), row3(mix_norm), row3(ffn_b_norm)
    g_qln, g_kvln = row3(mla_q_lora_norm), row3(mla_kv_lora_norm)

    xf = x.reshape(m, d)
    for l in range(depth):
        xf = _ffn(xf, g_a, *wa, l)

        proj = _proj(xf, g_mix, w_in_p, l)
        o_hg = _hgrn(proj, lb_all[l][None, :], hgrn_out_norm[l][None, :], hg_masks, hg_row_masks,
                     batch, seq)

        lambda_init = 0.8 - 0.6 * math.exp(-0.3 * l)
        lam = (jnp.exp(jnp.sum(diff_lambda_q1[l].astype(F32) * diff_lambda_k1[l].astype(F32)))
               - jnp.exp(jnp.sum(diff_lambda_q2[l].astype(F32) * diff_lambda_k2[l].astype(F32)))
               + lambda_init)
        qn, kn, vb = _df_prep(proj, jnp.tile(diff_q_norm[l], DF_W // DF_DQK)[None, :],
                              jnp.tile(diff_k_norm[l], DF_W // DF_DQK)[None, :], ones_blk)
        o_df = _df_attn(qn, kn, vb, b0, b1, far, jnp.full((1, DF_DV), lam, F32),
                        diff_subln[l][None, :], batch, seq, lambda_init)

        qg, kg = mla_q_norm[l], mla_k_norm[l]
        qh, kh, vm = _ml_prep(proj, g_qln, g_kvln, wqn, wqr, wkn, wv,
                              qg[None, :ML_NOPE], _spread_rope(qg[None, ML_NOPE:]),
                              kg[None, :ML_NOPE], _spread_rope(kg[None, ML_NOPE:]),
                              cos, sin, seq, l)
        o_ml = _ml_attn(qh, kh, vm, ml_mask, batch, seq)

        xf = _out_proj(xf, o_hg, o_df, o_ml, w_o1, w_o2, w_o3, l)
        xf = _ffn(xf, g_b, *wb, l)
    return xf.reshape(batch, seq, d)
```

```python
import functools
import math

import jax
import jax.numpy as jnp
import numpy as np
from jax import lax
from jax.experimental import pallas as pl
from jax.experimental.pallas import tpu as pltpu

F32 = jnp.float32
BF16 = jnp.bfloat16

D_MODEL = 2048
DEPTH = 4
CHUNK = 64
D_FF = 5504
EPS = 1e-6
HG_HEADS = 6
HG_DK = 128
HG_DV = 128
HG_W = HG_HEADS * HG_DK
DF_HEADS = 4
DF_DQK = 64
DF_DV = 128
DF_W = DF_HEADS * DF_DV
ML_HEADS = 6
ML_Q_RANK = 512
ML_KV_RANK = 256
ML_NOPE = 128
ML_ROPE = 64
ML_DV = 128
ML_DQK = ML_NOPE + ML_ROPE
ML_W = ML_HEADS * ML_DV
ROPE_BASE = 10000.0
REL_BUCKETS = 32
REL_MAX_DIST = 128
P_IN = 4 * HG_W + 3 * DF_W + ML_Q_RANK + ML_KV_RANK + ML_ROPE

LANES = 128
ML_HEAD_PAD = 2 * LANES
FF_TILE = 512
FF_SUB = 256
D_FF_PAD = -(-D_FF // FF_TILE) * FF_TILE
P_IN_PAD = -(-P_IN // FF_TILE) * FF_TILE
ROW_TILE = 1024
CAST_ROWS = 256
W_RING = 3
HG_ROWS = 512
ATT_TILE = 512
ATT_ROWS = 256
LOG2E = math.log2(math.e)
NEG_BIG = -1e30
VMEM_LIMIT = 56 * 1024 * 1024

_HG_COL = 0
_DF_COL = (4 * HG_W) // DF_W
_MLQ_COL = (4 * HG_W + 3 * DF_W) // ML_Q_RANK
_MLKV_COL = (4 * HG_W + 3 * DF_W + ML_Q_RANK) // ML_KV_RANK
_MLR_COL = (4 * HG_W + 3 * DF_W + ML_Q_RANK + ML_KV_RANK) // LANES


def _dot(a, b):
    return jnp.dot(a, b, preferred_element_type=F32)


def _dot_nt(a, b):
    return lax.dot_general(a, b, (((1,), (1,)), ((), ())), preferred_element_type=F32)


def _dot_tn(a, b):
    return lax.dot_general(a, b, (((0,), (0,)), ((), ())), preferred_element_type=F32)


def _params(*sem):
    return pltpu.CompilerParams(dimension_semantics=sem, vmem_limit_bytes=VMEM_LIMIT)


def _rms_rows(x, gain):
    return x * lax.rsqrt(jnp.mean(x * x, axis=-1, keepdims=True) + EPS) * gain


def _cast_pad_kernel(w_ref, o_ref):
    f = w_ref.shape[-1]
    o_ref[:, :f] = w_ref[...].astype(BF16)
    o_ref[:, f:] = jnp.zeros((o_ref.shape[0], o_ref.shape[1] - f), BF16)


def _cast_pad_cols(w, n):
    depth, d, f = w.shape
    assert f % LANES == 0 and n % LANES == 0 and n > f and d % CAST_ROWS == 0
    return pl.pallas_call(
        _cast_pad_kernel,
        grid=(depth, d // CAST_ROWS),
        in_specs=[pl.BlockSpec((None, CAST_ROWS, f), lambda l, i: (l, i, 0))],
        out_specs=pl.BlockSpec((None, CAST_ROWS, n), lambda l, i: (l, i, 0)),
        out_shape=jax.ShapeDtypeStruct((depth, d, n), BF16),
        compiler_params=_params("parallel", "parallel"),
        name="cast_pad",
    )(w)


def _ffn_kernel(x_ref, g_ref, wg_ref, wu_ref, wd_ref, o_ref, h_ref):
    @pl.when(pl.program_id(1) == 0)
    def _():
        x = x_ref[...]
        h_ref[...] = _rms_rows(x, g_ref[...]).astype(BF16)
        o_ref[...] = x

    h = h_ref[...]
    acts = []
    for c in range(0, FF_TILE, FF_SUB):
        a = _dot(h, wg_ref[:, c:c + FF_SUB])
        u = _dot(h, wu_ref[:, c:c + FF_SUB])
        acts.append(((0.5 * a) * jax.nn.sigmoid(a) * u).astype(BF16))
    o_ref[...] += _dot(jnp.concatenate(acts, axis=-1), wd_ref[...])


def _ffn(x, gains, wg, wu, wd, layer):
    m, d = x.shape
    return pl.pallas_call(
        _ffn_kernel,
        grid=(m // ROW_TILE, wg.shape[-1] // FF_TILE),
        in_specs=[
            pl.BlockSpec((ROW_TILE, d), lambda i, j: (i, 0)),
            pl.BlockSpec((None, 1, d), lambda i, j: (layer, 0, 0)),
            pl.BlockSpec((None, d, FF_TILE), lambda i, j: (layer, 0, j)),
            pl.BlockSpec((None, d, FF_TILE), lambda i, j: (layer, 0, j)),
            pl.BlockSpec((None, FF_TILE, d), lambda i, j: (layer, j, 0)),
        ],
        out_specs=pl.BlockSpec((ROW_TILE, d), lambda i, j: (i, 0)),
        out_shape=jax.ShapeDtypeStruct((m, d), F32),
        scratch_shapes=[pltpu.VMEM((ROW_TILE, d), BF16)],
        compiler_params=_params("parallel", "arbitrary"),
        name="ffn",
    )(x, gains, wg, wu, wd)


def _proj_kernel(x_ref, g_ref, w_hbm, o_ref, h_ref, w_buf, sem, *, layer):
    n_col = pl.num_programs(1)
    n_steps = pl.num_programs(0) * n_col
    step = pl.program_id(0) * n_col + pl.program_id(1)

    def tile_copy(s):
        col = pl.multiple_of(lax.rem(s, n_col) * FF_TILE, FF_TILE)
        slot = lax.rem(s, W_RING)
        return pltpu.make_async_copy(w_hbm.at[layer, :, pl.ds(col, FF_TILE)], w_buf.at[slot], sem.at[slot])

    @pl.when(step == 0)
    def _():
        for s in range(W_RING - 1):
            tile_copy(s).start()

    @pl.when(step + W_RING - 1 < n_steps)
    def _():
        tile_copy(step + W_RING - 1).start()

    @pl.when(pl.program_id(1) == 0)
    def _():
        h_ref[...] = _rms_rows(x_ref[...], g_ref[...]).astype(BF16)

    tile_copy(step).wait()
    o_ref[...] = _dot(h_ref[...], w_buf[lax.rem(step, W_RING)])


def _proj(x, gains, w, layer):
    m, d = x.shape
    n = w.shape[-1]
    assert (m // ROW_TILE) * (n // FF_TILE) >= W_RING
    return pl.pallas_call(
        functools.partial(_proj_kernel, layer=layer),
        grid=(m // ROW_TILE, n // FF_TILE),
        in_specs=[
            pl.BlockSpec((ROW_TILE, d), lambda i, j: (i, 0)),
            pl.BlockSpec((None, 1, d), lambda i, j: (layer, 0, 0)),
            pl.BlockSpec(memory_space=pl.ANY),
        ],
        out_specs=pl.BlockSpec((ROW_TILE, FF_TILE), lambda i, j: (i, j)),
        out_shape=jax.ShapeDtypeStruct((m, n), F32),
        scratch_shapes=[pltpu.VMEM((ROW_TILE, d), BF16), pltpu.VMEM((W_RING, d, FF_TILE), BF16),
                        pltpu.SemaphoreType.DMA((W_RING,))],
        compiler_params=_params("arbitrary", "arbitrary"),
        name="proj",
    )(x, gains, w)


def _hgrn_level_masks():
    t = np.arange(CHUNK)[:, None]
    s = np.arange(CHUNK)[None, :]
    levels = [(((t >> l) ^ (s >> l)) == 1) & (t > s) for l in range(6)]
    levels.append(t == s)
    return np.stack(levels).astype(np.float32)


def _hgrn_row_masks():
    t = np.arange(CHUNK)
    pats = [t % 2 == 1, t % 4 == 0, t % 4 != 0, t % 4 == 2, t % 4 == 3]
    pats += [(t >> l) & 1 == 1 for l in range(2, 6)]
    return np.broadcast_to(np.stack(pats)[:, :, None], (len(pats), CHUNK, LANES)).astype(np.float32)


_RM_ODD, _RM_K1, _RM_NK1, _RM_M2, _RM_M3, _RM_UPPER2 = 0, 1, 2, 3, 4, 5


def _block_bcast(p, level):
    n, w = p.shape
    half = 1 << level
    size = 2 * half
    assert size >= 8
    pieces = []
    for b in range(n // size):
        mrow = b * size + half - 1
        pieces.append(jnp.broadcast_to(p[mrow:mrow + 1, :], (size, w)))
    return jnp.concatenate(pieces, axis=0)


def _hgrn_kernel(q_ref, f_ref, i_ref, g_ref, lb_ref, og_ref, mask_ref, rm_ref, o_ref, st_ref):
    @pl.when(pl.program_id(1) == 0)
    def _():
        st_ref[...] = jnp.zeros_like(st_ref)

    og = og_ref[...]

    def head_chunk(rows, h):
        sl = slice(h * HG_DK, (h + 1) * HG_DK)
        lb = lb_ref[:, sl]
        q = q_ref[rows, sl] * (HG_DK ** -0.5)
        f = lb + (1.0 - lb) * jax.nn.sigmoid(f_ref[rows, sl])
        kk = 1.0 - f
        kb = kk.astype(BF16)
        v = i_ref[rows, sl].astype(BF16)
        gate = g_ref[rows, sl]
        p = jnp.log(f) * LOG2E
        q_dec = [f, None]
        k_dec = [None, rm_ref[_RM_K1] * pltpu.roll(f, CHUNK - 1, 0) + rm_ref[_RM_NK1]]
        p = p + rm_ref[_RM_ODD] * pltpu.roll(p, 1, 0)
        q_dec[1] = jnp.exp2(p)
        p = p + rm_ref[_RM_M2] * pltpu.roll(p, 1, 0) + rm_ref[_RM_M3] * pltpu.roll(p, 2, 0)
        for l in range(2, 6):
            tb = _block_bcast(p, l)
            q_dec.append(jnp.exp2(p))
            k_dec.append(jnp.exp2(jnp.minimum(tb - p, 0.0)))
            p = p + rm_ref[_RM_UPPER2 + l - 2] * tb
        last = p[CHUNK - 1:CHUNK, :]

        scores = mask_ref[6] * _dot_nt(q.astype(BF16), kb)
        for l in range(6):
            kl = kb if k_dec[l] is None else (kk * k_dec[l]).astype(BF16)
            scores += mask_ref[l] * _dot_nt((q * q_dec[l]).astype(BF16), kl)
        st = st_ref[h]
        o = _dot_nt((q * jnp.exp2(p)).astype(BF16), st.astype(BF16)) + _dot(scores.astype(BF16), v)
        k_out = (kk * jnp.exp2(last - p)).astype(BF16)
        st_ref[h] = st * jnp.exp2(last) + _dot_tn(v, k_out)
        o = _rms_rows(o, og) * (gate * jax.nn.sigmoid(gate))
        o_ref[rows, sl] = o.astype(o_ref.dtype)

    for c in range(q_ref.shape[0] // CHUNK):
        for h in range(HG_HEADS):
            head_chunk(slice(c * CHUNK, (c + 1) * CHUNK), h)


def _hgrn(proj, lb, og, masks, row_masks, batch, seq):
    m = proj.shape[0]
    ns = seq // HG_ROWS

    def col(c):
        return pl.BlockSpec((HG_ROWS, HG_W), lambda b, s: (b * ns + s, _HG_COL + c))

    return pl.pallas_call(
        _hgrn_kernel,
        grid=(batch, ns),
        in_specs=[col(0), col(1), col(2), col(3),
                  pl.BlockSpec((1, HG_W), lambda b, s: (0, 0)),
                  pl.BlockSpec((1, HG_DV), lambda b, s: (0, 0)),
                  pl.BlockSpec((7, CHUNK, CHUNK), lambda b, s: (0, 0, 0)),
                  pl.BlockSpec(row_masks.shape, lambda b, s: (0, 0, 0))],
        out_specs=pl.BlockSpec((HG_ROWS, HG_W), lambda b, s: (b * ns + s, 0)),
        out_shape=jax.ShapeDtypeStruct((m, HG_W), BF16),
        scratch_shapes=[pltpu.VMEM((HG_HEADS, HG_DV, HG_DK), F32)],
        compiler_params=_params("parallel", "arbitrary"),
        name="hgrn",
    )(proj, proj, proj, proj, lb, og, masks, row_masks)


def _group_sumsq(x, ones_blk):
    x2 = x * x
    hi = x2.astype(BF16)
    lo = (x2 - hi.astype(F32)).astype(BF16)
    return _dot(hi, ones_blk) + _dot(lo, ones_blk)


def _df_prep_kernel(q_ref, k_ref, v_ref, qg_ref, kg_ref, ones_ref, qo_ref, ko_ref, vo_ref):
    ones_blk = ones_ref[...]
    q = q_ref[...]
    k = k_ref[...]
    qn = q * lax.rsqrt(_group_sumsq(q, ones_blk) * (1.0 / DF_DQK) + EPS) * qg_ref[...]
    kn = k * lax.rsqrt(_group_sumsq(k, ones_blk) * (1.0 / DF_DQK) + EPS) * kg_ref[...]
    qo_ref[...] = (qn * (DF_DQK ** -0.5 * LOG2E)).astype(BF16)
    ko_ref[...] = kn.astype(BF16)
    vo_ref[...] = v_ref[...].astype(BF16)


def _df_prep(proj, qg, kg, ones_blk):
    m = proj.shape[0]

    def col(c):
        return pl.BlockSpec((ROW_TILE, DF_W), lambda i: (i, _DF_COL + c))

    row = pl.BlockSpec((ROW_TILE, DF_W), lambda i: (i, 0))
    vec = pl.BlockSpec((1, DF_W), lambda i: (0, 0))
    out = jax.ShapeDtypeStruct((m, DF_W), BF16)
    return pl.pallas_call(
        _df_prep_kernel,
        grid=(m // ROW_TILE,),
        in_specs=[col(0), col(1), col(2), vec, vec, pl.BlockSpec((DF_W, DF_W), lambda i: (0, 0))],
        out_specs=[row, row, row],
        out_shape=[out, out, out],
        compiler_params=_params("parallel"),
        name="df_prep",
    )(proj, proj, proj, qg, kg, ones_blk)


def _attn_kernel(*refs, n_maps, has_bias, finish):
    if has_bias:
        q_ref, k_ref, v_ref, b0_ref, b1_ref, far_ref, *rest = refs
    else:
        q_ref, k_ref, v_ref, b0_ref, *rest = refs
        b1_ref = far_ref = None
    *extra, o_ref, m_ref, l_ref, acc_ref, s_ref = rest
    i = pl.program_id(2)
    t = q_ref.shape[0]
    nc = t // LANES
    if n_maps == 2:
        q = q_ref[...]
        lane = lax.broadcasted_iota(jnp.int32, q.shape, 1)
        zero = jnp.zeros_like(q)
        qs = [jnp.where(lane < DF_DQK, q, zero), jnp.where(lane >= DF_DQK, q, zero)]
    else:
        qs = [q_ref[...]]
    q_all = qs[0] if n_maps == 1 else jnp.concatenate(qs, axis=0)

    m_ref[...] = jnp.full_like(m_ref, NEG_BIG)
    l_ref[...] = jnp.zeros_like(l_ref)
    acc_ref[...] = jnp.zeros_like(acc_ref)

    chains = [(mi, slice(r0, r0 + ATT_ROWS)) for mi in range(n_maps) for r0 in range(0, t, ATT_ROWS)]

    def key_rows(n):
        return pl.ds(pl.multiple_of(n * t, t), t)

    def visible(r, diag):
        return r.stop // LANES if diag else nc

    def logits(n, slot, diag=False):
        kt = k_ref[key_rows(n), :]
        if not diag:
            s = _dot_nt(q_all, kt)
            for mi in range(n_maps):
                s_ref[slot, mi] = s[mi * t:(mi + 1) * t]
            return
        for mi, r in chains:
            w = visible(r, diag) * LANES
            s_ref[slot, mi, r, :w] = _dot_nt(qs[mi][r], kt[:w])

    def consume(n, slot, bias_ref, const, diag=False):
        vt = v_ref[key_rows(n), :]
        for mi, r in chains:
            def col(c):
                x = s_ref[slot, mi, r, c * LANES:(c + 1) * LANES]
                return x if bias_ref is None else x + bias_ref[0, r, c * LANES:(c + 1) * LANES]

            n_cols = visible(r, diag)
            m_prev = m_ref[mi, r]
            tile_max = functools.reduce(jnp.maximum, [col(c) for c in range(n_cols)])
            row_max = jnp.max(tile_max, axis=-1, keepdims=True)
            if const is None:
                m_new = jnp.maximum(m_prev, row_max)
                shift = m_new
            else:
                m_new = jnp.maximum(m_prev, row_max + const)
                shift = m_new - const
            m_ref[mi, r] = m_new
            alpha = jnp.exp2(m_prev - m_new)
            ps = [jnp.exp2(col(c) - shift) for c in range(n_cols)]
            l_ref[mi, r] = alpha * l_ref[mi, r] + functools.reduce(jnp.add, ps)
            p = jnp.concatenate([x.astype(BF16) for x in ps], axis=-1)
            acc_ref[mi, r] = alpha * acc_ref[mi, r] + _dot(p, vt[:p.shape[1]])

    far = far_ref[0, 0:1, 0:1] if has_bias else None
    prev_bias = b1_ref if has_bias else None

    n_far = jnp.maximum(i - 1, 0)
    n_pairs = lax.shift_right_logical(n_far, 1)
    logits(0, 0)

    def pair_body(j, carry):
        n = 2 * j
        logits(n + 1, 1)
        consume(n, 0, None, far)
        logits(n + 2, 0)
        consume(n + 1, 1, None, far)
        return carry

    lax.fori_loop(0, n_pairs, pair_body, 0)

    @pl.when(i == 0)
    def _():
        consume(0, 0, b0_ref, None, diag=True)

    @pl.when((i >= 1) & ((n_far & 1) == 0))
    def _():
        logits(i, 1, diag=True)
        consume(i - 1, 0, prev_bias, None)
        consume(i, 1, b0_ref, None, diag=True)

    @pl.when((n_far & 1) == 1)
    def _():
        logits(i - 1, 1)
        consume(i - 2, 0, None, far)
        logits(i, 0, diag=True)
        consume(i - 1, 1, prev_bias, None)
        consume(i, 0, b0_ref, None, diag=True)

    outs = [acc_ref[mi] / jnp.sum(l_ref[mi], axis=-1, keepdims=True) for mi in range(n_maps)]
    o_ref[...] = finish(outs, *extra).astype(o_ref.dtype)


def _df_finish(outs, lam_ref, sg_ref, *, lambda_init):
    o = outs[0] - lam_ref[...] * outs[1]
    return _rms_rows(o, sg_ref[...]) * (1.0 - lambda_init)


def _ml_finish(outs):
    return outs[0]


def _df_attn(qn, kn, vb, b0, b1, far, lam, sg, batch, seq, lambda_init):
    m = qn.shape[0]
    t = ATT_TILE
    nq = seq // t
    kv = pl.BlockSpec((seq, DF_DV), lambda b, h, i: (b, h))
    tile = pl.BlockSpec((1, t, t), lambda b, h, i: (h, 0, 0))
    vec = pl.BlockSpec((1, DF_DV), lambda b, h, i: (0, 0))
    kern = functools.partial(_attn_kernel, n_maps=2, has_bias=True,
                             finish=functools.partial(_df_finish, lambda_init=lambda_init))
    return pl.pallas_call(
        kern,
        grid=(batch, DF_HEADS, nq),
        in_specs=[pl.BlockSpec((t, DF_DV), lambda b, h, i: (b * nq + i, h)), kv, kv, tile, tile,
                  pl.BlockSpec((1, 8, LANES), lambda b, h, i: (h, 0, 0)), vec, vec],
        out_specs=pl.BlockSpec((t, DF_DV), lambda b, h, i: (b * nq + i, h)),
        out_shape=jax.ShapeDtypeStruct((m, DF_W), BF16),
        scratch_shapes=[pltpu.VMEM((2, t, LANES), F32), pltpu.VMEM((2, t, LANES), F32),
                        pltpu.VMEM((2, t, DF_DV), F32), pltpu.VMEM((2, 2, t, t), F32)],
        compiler_params=_params("parallel", "parallel", "arbitrary"),
        name="df_attn",
    )(qn, kn, vb, b0, b1, far, lam, sg)


def _ml_attn(qh, kh, vb, mask, batch, seq):
    m = qh.shape[0]
    t = ATT_TILE
    nq = seq // t
    return pl.pallas_call(
        functools.partial(_attn_kernel, n_maps=1, has_bias=False, finish=_ml_finish),
        grid=(batch, ML_HEADS, nq),
        in_specs=[pl.BlockSpec((t, ML_HEAD_PAD), lambda b, h, i: (b * nq + i, h)),
                  pl.BlockSpec((seq, ML_HEAD_PAD), lambda b, h, i: (b, h)),
                  pl.BlockSpec((seq, ML_DV), lambda b, h, i: (b, h)),
                  pl.BlockSpec((1, t, t), lambda b, h, i: (0, 0, 0))],
        out_specs=pl.BlockSpec((t, ML_DV), lambda b, h, i: (b * nq + i, h)),
        out_shape=jax.ShapeDtypeStruct((m, ML_W), BF16),
        scratch_shapes=[pltpu.VMEM((1, t, LANES), F32), pltpu.VMEM((1, t, LANES), F32),
                        pltpu.VMEM((1, t, ML_DV), F32), pltpu.VMEM((2, 1, t, t), F32)],
        compiler_params=_params("parallel", "parallel", "arbitrary"),
        name="ml_attn",
    )(qh, kh, vb, mask)


def _rope_pad(x, cos, sin):
    return x * cos + pltpu.roll(x, LANES // 2, 1) * sin


def _ml_prep_kernel(cq_ref, ckv_ref, kr_ref, qln_ref, kvln_ref, wqn_ref, wqr_ref, wkn_ref, wv_ref,
                    qgn_ref, qgr_ref, kgn_ref, kgr_ref, cos_ref, sin_ref, qo_ref, ko_ref, vo_ref):
    cos = cos_ref[...]
    sin = sin_ref[...]
    cq = _rms_rows(cq_ref[...], qln_ref[...]).astype(BF16)
    ckv = _rms_rows(ckv_ref[...], kvln_ref[...]).astype(BF16)
    q_nope = _dot(cq, wqn_ref[...])
    q_rope = _dot(cq, wqr_ref[...])
    k_nope = _dot(ckv, wkn_ref[...])
    vo_ref[...] = _dot(ckv, wv_ref[...]).astype(BF16)
    k_raw = kr_ref[...]
    lane = lax.broadcasted_iota(jnp.int32, k_raw.shape, 1)
    k_lo = jnp.where(lane < ML_ROPE // 2, k_raw, 0.0)
    k_rope = k_lo + pltpu.roll(k_raw - k_lo, ML_ROPE // 2, 1)
    kr_sq = k_rope * k_rope
    kr_rot = _rope_pad(k_rope * kgr_ref[...], cos, sin)
    qgn, qgr, kgn = qgn_ref[...], qgr_ref[...], kgn_ref[...]
    for h in range(ML_HEADS):
        sl = slice(h * LANES, (h + 1) * LANES)
        qn, qr, kn = q_nope[:, sl], q_rope[:, sl], k_nope[:, sl]
        q_ss = jnp.sum(qn * qn + qr * qr, axis=-1, keepdims=True)
        q_rs = lax.rsqrt(q_ss * (1.0 / ML_DQK) + EPS) * (ML_DQK ** -0.5 * LOG2E)
        k_ss = jnp.sum(kn * kn + kr_sq, axis=-1, keepdims=True)
        k_rs = lax.rsqrt(k_ss * (1.0 / ML_DQK) + EPS)
        base = h * ML_HEAD_PAD
        qo_ref[:, base:base + LANES] = (qn * q_rs * qgn).astype(BF16)
        qo_ref[:, base + LANES:base + 2 * LANES] = (_rope_pad(qr * qgr, cos, sin) * q_rs).astype(BF16)
        ko_ref[:, base:base + LANES] = (kn * k_rs * kgn).astype(BF16)
        ko_ref[:, base + LANES:base + 2 * LANES] = (kr_rot * k_rs).astype(BF16)


def _ml_prep(proj, qln, kvln, wqn, wqr, wkn, wv, qgn, qgr, kgn, kgr, cos, sin, seq, layer):
    m = proj.shape[0]
    npos = seq // ROW_TILE

    def full(a):
        nd = a.ndim
        if nd == 3:
            return pl.BlockSpec((None,) + a.shape[1:], lambda i: (layer, 0, 0))
        return pl.BlockSpec(a.shape, lambda i: (0, 0))

    pos = pl.BlockSpec((ROW_TILE, LANES), lambda i: (i % npos, 0))
    hp = ML_HEADS * ML_HEAD_PAD
    return pl.pallas_call(
        _ml_prep_kernel,
        grid=(m // ROW_TILE,),
        in_specs=[pl.BlockSpec((ROW_TILE, ML_Q_RANK), lambda i: (i, _MLQ_COL)),
                  pl.BlockSpec((ROW_TILE, ML_KV_RANK), lambda i: (i, _MLKV_COL)),
                  pl.BlockSpec((ROW_TILE, LANES), lambda i: (i, _MLR_COL)),
                  full(qln), full(kvln), full(wqn), full(wqr), full(wkn), full(wv),
                  full(qgn), full(qgr), full(kgn), full(kgr), pos, pos],
        out_specs=[pl.BlockSpec((ROW_TILE, hp), lambda i: (i, 0)),
                   pl.BlockSpec((ROW_TILE, hp), lambda i: (i, 0)),
                   pl.BlockSpec((ROW_TILE, ML_W), lambda i: (i, 0))],
        out_shape=[jax.ShapeDtypeStruct((m, hp), BF16), jax.ShapeDtypeStruct((m, hp), BF16),
                   jax.ShapeDtypeStruct((m, ML_W), BF16)],
        compiler_params=_params("parallel"),
        name="ml_prep",
    )(proj, proj, proj, qln, kvln, wqn, wqr, wkn, wv, qgn, qgr, kgn, kgr, cos, sin)


def _out_kernel(x_ref, hg_ref, df_ref, ml_ref, w1_ref, w2_ref, w3_ref, o_ref):
    o_ref[...] = (x_ref[...] + _dot(hg_ref[...], w1_ref[...]) + _dot(df_ref[...], w2_ref[...])
                  + _dot(ml_ref[...], w3_ref[...]))


def _out_proj(x, o_hg, o_df, o_ml, w1, w2, w3, layer):
    m, d = x.shape

    def rows(w):
        return pl.BlockSpec((ROW_TILE, w), lambda i: (i, 0))

    def wspec(w):
        return pl.BlockSpec((None,) + w.shape[1:], lambda i: (layer, 0, 0))

    return pl.pallas_call(
        _out_kernel,
        grid=(m // ROW_TILE,),
        in_specs=[rows(d), rows(HG_W), rows(DF_W), rows(ML_W), wspec(w1), wspec(w2), wspec(w3)],
        out_specs=rows(d),
        out_shape=jax.ShapeDtypeStruct((m, d), F32),
        compiler_params=_params("parallel"),
        name="out_proj",
    )(x, o_hg, o_df, o_ml, w1, w2, w3)


def _t5_bucket(rel):
    half = REL_BUCKETS // 2
    max_exact = half // 2
    ret = (rel > 0).astype(jnp.int32) * half
    n = jnp.abs(rel)
    large = max_exact + (jnp.log(jnp.maximum(n, 1).astype(F32) / max_exact)
                         / math.log(REL_MAX_DIST / max_exact) * (half - max_exact)).astype(jnp.int32)
    large = jnp.minimum(large, half - 1)
    return ret + jnp.where(n < max_exact, n, large)


def _chunk_mask_tile(t):
    ii = jnp.arange(t)
    return (ii[None, :] // CHUNK) <= (ii[:, None] // CHUNK)


def _bias_tiles(rel_bias, t):
    assert t >= REL_MAX_DIST and t % CHUNK == 0
    table = rel_bias.astype(F32)

    def lookup(bucket):
        out = jnp.zeros((DF_HEADS,) + bucket.shape, F32)
        for b in range(REL_BUCKETS):
            out = out + jnp.where(bucket[None] == b, table[b][:, None, None], 0.0)
        return out

    ii = jnp.arange(t)
    rel = ii[None, :] - ii[:, None]
    b0 = jnp.where(_chunk_mask_tile(t)[None], lookup(_t5_bucket(rel)) * LOG2E, NEG_BIG)
    b1 = lookup(_t5_bucket(rel - t)) * LOG2E
    far = table[_t5_bucket(jnp.full((), -2 * t, jnp.int32))] * LOG2E
    far = jnp.broadcast_to(far[:, None, None], (DF_HEADS, 8, LANES))
    return b0, b1, far


def _rope_tables(seq):
    r = ML_ROPE
    freqs = ROPE_BASE ** (-jnp.arange(0, r, 2, dtype=F32) / r)
    ang = jnp.arange(seq).astype(F32)[:, None] * freqs[None, :]
    cos, sin = jnp.cos(ang), jnp.sin(ang)
    return (_spread_rope(jnp.concatenate([cos, cos], axis=-1)),
            _spread_rope(jnp.concatenate([-sin, sin], axis=-1)))


def _spread_rope(a):
    half = ML_ROPE // 2
    z = jnp.zeros(a.shape[:-1] + (LANES // 2 - half,), a.dtype)
    return jnp.concatenate([a[..., :half], z, a[..., half:], z], axis=-1)


def _pad_axis(a, n, axis):
    shape = list(a.shape)
    shape[axis] = n - a.shape[axis]
    return jnp.concatenate([a, jnp.zeros(shape, a.dtype)], axis=axis)


def _pad_last(a, n):
    return _pad_axis(a, n, a.ndim - 1)


def _spread_rope_heads(a):
    lead = a.shape[:-1]
    return _spread_rope(a.reshape(lead + (ML_HEADS, ML_ROPE))).reshape(lead + (ML_HEADS * LANES,))


def kernel(x, ffn_a_norm, ffn_a_w_gate, ffn_a_w_up, ffn_a_w_down, mix_norm, w_in, w_out, hgrn_lb_logits, hgrn_out_norm, diff_q_norm, diff_k_norm, diff_lambda_q1, diff_lambda_k1, diff_lambda_q2, diff_lambda_k2, diff_subln, rel_bias, mla_q_lora_norm, mla_w_uq, mla_kv_lora_norm, mla_w_ukv, mla_q_norm, mla_k_norm, ffn_b_norm, ffn_b_w_gate, ffn_b_w_up, ffn_b_w_down):
    batch, seq, d = x.shape
    depth = w_in.shape[0]
    m = batch * seq
    assert seq % ROW_TILE == 0 and seq % ATT_TILE == 0 and seq % HG_ROWS == 0

    def ffn_weights(wg, wu, wd):
        return (_cast_pad_cols(wg, D_FF_PAD), _cast_pad_cols(wu, D_FF_PAD),
                _pad_axis(wd.astype(BF16), D_FF_PAD, 1))

    wa = ffn_weights(ffn_a_w_gate, ffn_a_w_up, ffn_a_w_down)
    wb = ffn_weights(ffn_b_w_gate, ffn_b_w_up, ffn_b_w_down)
    w_in_p = _pad_last(w_in.astype(BF16), P_IN_PAD)
    w_o = w_out.astype(BF16)
    w_o1, w_o2, w_o3 = w_o[:, :HG_W], w_o[:, HG_W:HG_W + DF_W], w_o[:, HG_W + DF_W:]

    uq = mla_w_uq.reshape(depth, ML_Q_RANK, ML_HEADS, ML_DQK)
    wqn = uq[..., :ML_NOPE].reshape(depth, ML_Q_RANK, ML_HEADS * ML_NOPE).astype(BF16)
    wqr = _spread_rope_heads(uq[..., ML_NOPE:].reshape(depth, ML_Q_RANK, ML_HEADS * ML_ROPE)).astype(BF16)
    ukv = mla_w_ukv.reshape(depth, ML_KV_RANK, ML_HEADS, ML_NOPE + ML_DV)
    wkn = ukv[..., :ML_NOPE].reshape(depth, ML_KV_RANK, ML_HEADS * ML_NOPE).astype(BF16)
    wv = ukv[..., ML_NOPE:].reshape(depth, ML_KV_RANK, ML_HEADS * ML_DV).astype(BF16)

    lb_all = jnp.cumsum(jax.nn.softmax(hgrn_lb_logits.astype(F32), axis=0), axis=0)
    lb_all = lb_all - lb_all[0:1]

    hg_masks = jnp.asarray(_hgrn_level_masks())
    hg_row_masks = jnp.asarray(_hgrn_row_masks())
    b0, b1, far = _bias_tiles(rel_bias, ATT_TILE)
    ml_mask = jnp.where(_chunk_mask_tile(ATT_TILE), 0.0, NEG_BIG).astype(F32)[None]
    cos, sin = _rope_tables(seq)
    ones_blk = jnp.asarray(np.kron(np.eye(DF_W // DF_DQK), np.ones((DF_DQK, DF_DQK))), BF16)

    def row3(a):
        return a.reshape(a.shape[0], 1, a.shape[1])

    g_a, g_mix, g_b = row3(ffn_a_norm), row3(mix_norm), row3(ffn_b_norm)
    g_qln, g_kvln = row3(mla_q_lora_norm), row3(mla_kv_lora_norm)

    xf = x.reshape(m, d)
    for l in range(depth):
        xf = _ffn(xf, g_a, *wa, l)

        proj = _proj(xf, g_mix, w_in_p, l)
        o_hg = _hgrn(proj, lb_all[l][None, :], hgrn_out_norm[l][None, :], hg_masks, hg_row_masks,
                     batch, seq)

        lambda_init = 0.8 - 0.6 * math.exp(-0.3 * l)
        lam = (jnp.exp(jnp.sum(diff_lambda_q1[l].astype(F32) * diff_lambda_k1[l].astype(F32)))
               - jnp.exp(jnp.sum(diff_lambda_q2[l].astype(F32) * diff_lambda_k2[l].astype(F32)))
               + lambda_init)
        qn, kn, vb = _df_prep(proj, jnp.tile(diff_q_norm[l], DF_W // DF_DQK)[None, :],
                              jnp.tile(diff_k_norm[l], DF_W // DF_DQK)[None, :], ones_blk)
        o_df = _df_attn(qn, kn, vb, b0, b1, far, jnp.full((1, DF_DV), lam, F32),
                        diff_subln[l][None, :], batch, seq, lambda_init)

        qg, kg = mla_q_norm[l], mla_k_norm[l]
        qh, kh, vm = _ml_prep(proj, g_qln, g_kvln, wqn, wqr, wkn, wv,
                              qg[None, :ML_NOPE], _spread_rope(qg[None, ML_NOPE:]),
                              kg[None, :ML_NOPE], _spread_rope(kg[None, ML_NOPE:]),
                              cos, sin, seq, l)
        o_ml = _ml_attn(qh, kh, vm, ml_mask, batch, seq)

        xf = _out_proj(xf, o_hg, o_df, o_ml, w_o1, w_o2, w_o3, l)
        xf = _ffn(xf, g_b, *wb, l)
    return xf.reshape(batch, seq, d)
```

```python
import functools
import math

import jax
import jax.numpy as jnp
import numpy as np
from jax import lax
from jax.experimental import pallas as pl
from jax.experimental.pallas import tpu as pltpu

F32 = jnp.float32
BF16 = jnp.bfloat16

D_MODEL = 2048
DEPTH = 4
CHUNK = 64
D_FF = 5504
EPS = 1e-6
HG_HEADS = 6
HG_DK = 128
HG_DV = 128
HG_W = HG_HEADS * HG_DK
DF_HEADS = 4
DF_DQK = 64
DF_DV = 128
DF_W = DF_HEADS * DF_DV
ML_HEADS = 6
ML_Q_RANK = 512
ML_KV_RANK = 256
ML_NOPE = 128
ML_ROPE = 64
ML_DV = 128
ML_DQK = ML_NOPE + ML_ROPE
ML_W = ML_HEADS * ML_DV
ROPE_BASE = 10000.0
REL_BUCKETS = 32
REL_MAX_DIST = 128
P_IN = 4 * HG_W + 3 * DF_W + ML_Q_RANK + ML_KV_RANK + ML_ROPE

LANES = 128
ML_HEAD_PAD = 2 * LANES
FF_TILE = 512
FF_SUB = 256
D_FF_PAD = -(-D_FF // FF_TILE) * FF_TILE
P_IN_PAD = -(-P_IN // FF_TILE) * FF_TILE
ROW_TILE = 1024
CAST_ROWS = 256
HG_ROWS = 512
ATT_TILE = 512
ATT_ROWS = 256
LOG2E = math.log2(math.e)
NEG_BIG = -1e30
VMEM_LIMIT = 56 * 1024 * 1024

_HG_TILES = 4
_DM_FIRST = (_HG_TILES * HG_W) // DF_W
_DM_TILES = (P_IN_PAD - _HG_TILES * HG_W) // DF_W
_DF_TILE, _MLQ_TILE, _MLKV_TILE = 0, 3, 4
_MLR_LANE_BLOCK = ML_KV_RANK // LANES
assert DF_W == ML_Q_RANK and _DM_TILES == 5


def _dot(a, b):
    return jnp.dot(a, b, preferred_element_type=F32)


def _dot_nt(a, b):
    return lax.dot_general(a, b, (((1,), (1,)), ((), ())), preferred_element_type=F32)


def _dot_tn(a, b):
    return lax.dot_general(a, b, (((0,), (0,)), ((), ())), preferred_element_type=F32)


def _params(*sem):
    return pltpu.CompilerParams(dimension_semantics=sem, vmem_limit_bytes=VMEM_LIMIT)


def _rms_rows(x, gain):
    return x * lax.rsqrt(jnp.mean(x * x, axis=-1, keepdims=True) + EPS) * gain


def _cast_pad_kernel(w_ref, o_ref):
    f = w_ref.shape[-1]
    o_ref[:, :f] = w_ref[...].astype(BF16)
    o_ref[:, f:] = jnp.zeros((o_ref.shape[0], o_ref.shape[1] - f), BF16)


def _cast_pad_cols(w, n):
    depth, d, f = w.shape
    assert f % LANES == 0 and n % LANES == 0 and n > f and d % CAST_ROWS == 0
    return pl.pallas_call(
        _cast_pad_kernel,
        grid=(depth, d // CAST_ROWS),
        in_specs=[pl.BlockSpec((None, CAST_ROWS, f), lambda l, i: (l, i, 0))],
        out_specs=pl.BlockSpec((None, CAST_ROWS, n), lambda l, i: (l, i, 0)),
        out_shape=jax.ShapeDtypeStruct((depth, d, n), BF16),
        compiler_params=_params("parallel", "parallel"),
        name="cast_pad",
    )(w)


def _ffn_kernel(x_ref, g_ref, wg_ref, wu_ref, wd_ref, o_ref, h_ref):
    @pl.when(pl.program_id(1) == 0)
    def _():
        x = x_ref[...]
        h_ref[...] = _rms_rows(x, g_ref[...]).astype(BF16)
        o_ref[...] = x

    h = h_ref[...]
    acts = []
    for c in range(0, FF_TILE, FF_SUB):
        a = _dot(h, wg_ref[:, c:c + FF_SUB])
        u = _dot(h, wu_ref[:, c:c + FF_SUB])
        acts.append(((0.5 * a) * jax.nn.sigmoid(a) * u).astype(BF16))
    o_ref[...] += _dot(jnp.concatenate(acts, axis=-1), wd_ref[...])


def _ffn(x, gains, wg, wu, wd, layer):
    m, d = x.shape
    return pl.pallas_call(
        _ffn_kernel,
        grid=(m // ROW_TILE, wg.shape[-1] // FF_TILE),
        in_specs=[
            pl.BlockSpec((ROW_TILE, d), lambda i, j: (i, 0)),
            pl.BlockSpec((None, 1, d), lambda i, j: (layer, 0, 0)),
            pl.BlockSpec((None, d, FF_TILE), lambda i, j: (layer, 0, j)),
            pl.BlockSpec((None, d, FF_TILE), lambda i, j: (layer, 0, j)),
            pl.BlockSpec((None, FF_TILE, d), lambda i, j: (layer, j, 0)),
        ],
        out_specs=pl.BlockSpec((ROW_TILE, d), lambda i, j: (i, 0)),
        out_shape=jax.ShapeDtypeStruct((m, d), F32),
        scratch_shapes=[pltpu.VMEM((ROW_TILE, d), BF16)],
        compiler_params=_params("parallel", "arbitrary"),
        name="ffn",
    )(x, gains, wg, wu, wd)


def _proj_kernel(x_ref, g_ref, w_ref, o_ref, h_ref):
    @pl.when(pl.program_id(1) == 0)
    def _():
        h_ref[...] = _rms_rows(x_ref[...], g_ref[...]).astype(BF16)

    o_ref[...] = _dot(h_ref[...], w_ref[...])


def _proj(x, gains, w, layer, first_tile, n_tiles, width):
    m, d = x.shape
    return pl.pallas_call(
        _proj_kernel,
        grid=(m // ROW_TILE, n_tiles),
        in_specs=[
            pl.BlockSpec((ROW_TILE, d), lambda i, j: (i, 0)),
            pl.BlockSpec((None, 1, d), lambda i, j: (layer, 0, 0)),
            pl.BlockSpec((None, d, width), lambda i, j: (layer, 0, first_tile + j)),
        ],
        out_specs=pl.BlockSpec((None, ROW_TILE, width), lambda i, j: (j, i, 0)),
        out_shape=jax.ShapeDtypeStruct((n_tiles, m, width), F32),
        scratch_shapes=[pltpu.VMEM((ROW_TILE, d), BF16)],
        compiler_params=_params("parallel", "arbitrary"),
        name="proj",
    )(x, gains, w)


def _hgrn_level_masks():
    t = np.arange(CHUNK)[:, None]
    s = np.arange(CHUNK)[None, :]
    levels = [(((t >> l) ^ (s >> l)) == 1) & (t > s) for l in range(6)]
    levels.append(t == s)
    return np.stack(levels).astype(np.float32)


def _hgrn_row_masks():
    t = np.arange(CHUNK)
    pats = [t % 2 == 1, t % 4 == 0, t % 4 != 0, t % 4 == 2, t % 4 == 3]
    pats += [(t >> l) & 1 == 1 for l in range(2, 6)]
    return np.broadcast_to(np.stack(pats)[:, :, None], (len(pats), CHUNK, LANES)).astype(np.float32)


_RM_ODD, _RM_K1, _RM_NK1, _RM_M2, _RM_M3, _RM_UPPER2 = 0, 1, 2, 3, 4, 5


def _block_bcast(p, level):
    n, w = p.shape
    half = 1 << level
    size = 2 * half
    assert size >= 8
    pieces = []
    for b in range(n // size):
        mrow = b * size + half - 1
        pieces.append(jnp.broadcast_to(p[mrow:mrow + 1, :], (size, w)))
    return jnp.concatenate(pieces, axis=0)


def _hgrn_kernel(q_ref, f_ref, i_ref, g_ref, lb_ref, og_ref, mask_ref, rm_ref, o_ref, st_ref):
    @pl.when(pl.program_id(1) == 0)
    def _():
        st_ref[...] = jnp.zeros_like(st_ref)

    og = og_ref[...]

    def head_chunk(rows, h):
        sl = slice(h * HG_DK, (h + 1) * HG_DK)
        lb = lb_ref[:, sl]
        q = q_ref[rows, sl] * (HG_DK ** -0.5)
        f = lb + (1.0 - lb) * jax.nn.sigmoid(f_ref[rows, sl])
        kk = 1.0 - f
        kb = kk.astype(BF16)
        v = i_ref[rows, sl].astype(BF16)
        gate = g_ref[rows, sl]
        p = jnp.log(f) * LOG2E
        q_dec = [f, None]
        k_dec = [None, rm_ref[_RM_K1] * pltpu.roll(f, CHUNK - 1, 0) + rm_ref[_RM_NK1]]
        p = p + rm_ref[_RM_ODD] * pltpu.roll(p, 1, 0)
        q_dec[1] = jnp.exp2(p)
        p = p + rm_ref[_RM_M2] * pltpu.roll(p, 1, 0) + rm_ref[_RM_M3] * pltpu.roll(p, 2, 0)
        for l in range(2, 6):
            tb = _block_bcast(p, l)
            q_dec.append(jnp.exp2(p))
            k_dec.append(jnp.exp2(jnp.minimum(tb - p, 0.0)))
            p = p + rm_ref[_RM_UPPER2 + l - 2] * tb
        last = p[CHUNK - 1:CHUNK, :]

        scores = mask_ref[6] * _dot_nt(q.astype(BF16), kb)
        for l in range(6):
            kl = kb if k_dec[l] is None else (kk * k_dec[l]).astype(BF16)
            scores += mask_ref[l] * _dot_nt((q * q_dec[l]).astype(BF16), kl)
        st = st_ref[h]
        o = _dot_nt((q * jnp.exp2(p)).astype(BF16), st.astype(BF16)) + _dot(scores.astype(BF16), v)
        k_out = (kk * jnp.exp2(last - p)).astype(BF16)
        st_ref[h] = st * jnp.exp2(last) + _dot_tn(v, k_out)
        o = _rms_rows(o, og) * (gate * jax.nn.sigmoid(gate))
        o_ref[rows, sl] = o.astype(o_ref.dtype)

    for c in range(q_ref.shape[0] // CHUNK):
        for h in range(HG_HEADS):
            head_chunk(slice(c * CHUNK, (c + 1) * CHUNK), h)


def _hgrn(proj, lb, og, masks, row_masks, batch, seq):
    m = proj.shape[1]
    ns = seq // HG_ROWS

    def col(c):
        return pl.BlockSpec((None, HG_ROWS, HG_W), lambda b, s: (c, b * ns + s, 0))

    return pl.pallas_call(
        _hgrn_kernel,
        grid=(batch, ns),
        in_specs=[col(0), col(1), col(2), col(3),
                  pl.BlockSpec((1, HG_W), lambda b, s: (0, 0)),
                  pl.BlockSpec((1, HG_DV), lambda b, s: (0, 0)),
                  pl.BlockSpec((7, CHUNK, CHUNK), lambda b, s: (0, 0, 0)),
                  pl.BlockSpec(row_masks.shape, lambda b, s: (0, 0, 0))],
        out_specs=pl.BlockSpec((HG_ROWS, HG_W), lambda b, s: (b * ns + s, 0)),
        out_shape=jax.ShapeDtypeStruct((m, HG_W), BF16),
        scratch_shapes=[pltpu.VMEM((HG_HEADS, HG_DV, HG_DK), F32)],
        compiler_params=_params("parallel", "arbitrary"),
        name="hgrn",
    )(proj, proj, proj, proj, lb, og, masks, row_masks)


def _group_sumsq(x, ones_blk):
    x2 = x * x
    hi = x2.astype(BF16)
    lo = (x2 - hi.astype(F32)).astype(BF16)
    return _dot(hi, ones_blk) + _dot(lo, ones_blk)


def _df_prep_kernel(q_ref, k_ref, v_ref, qg_ref, kg_ref, ones_ref, qo_ref, ko_ref, vo_ref):
    ones_blk = ones_ref[...]
    q = q_ref[...]
    k = k_ref[...]
    qn = q * lax.rsqrt(_group_sumsq(q, ones_blk) * (1.0 / DF_DQK) + EPS) * qg_ref[...]
    kn = k * lax.rsqrt(_group_sumsq(k, ones_blk) * (1.0 / DF_DQK) + EPS) * kg_ref[...]
    qo_ref[...] = (qn * (DF_DQK ** -0.5 * LOG2E)).astype(BF16)
    ko_ref[...] = kn.astype(BF16)
    vo_ref[...] = v_ref[...].astype(BF16)


def _df_prep(proj, qg, kg, ones_blk):
    m = proj.shape[1]

    def col(c):
        return pl.BlockSpec((None, ROW_TILE, DF_W), lambda i: (_DF_TILE + c, i, 0))

    row = pl.BlockSpec((ROW_TILE, DF_W), lambda i: (i, 0))
    vec = pl.BlockSpec((1, DF_W), lambda i: (0, 0))
    out = jax.ShapeDtypeStruct((m, DF_W), BF16)
    return pl.pallas_call(
        _df_prep_kernel,
        grid=(m // ROW_TILE,),
        in_specs=[col(0), col(1), col(2), vec, vec, pl.BlockSpec((DF_W, DF_W), lambda i: (0, 0))],
        out_specs=[row, row, row],
        out_shape=[out, out, out],
        compiler_params=_params("parallel"),
        name="df_prep",
    )(proj, proj, proj, qg, kg, ones_blk)


def _attn_kernel(*refs, n_maps, has_bias, finish):
    if has_bias:
        q_ref, k_ref, v_ref, b0_ref, b1_ref, far_ref, *rest = refs
    else:
        q_ref, k_ref, v_ref, b0_ref, *rest = refs
        b1_ref = far_ref = None
    *extra, o_ref, m_ref, l_ref, acc_ref, s_ref = rest
    i = pl.program_id(2)
    t = q_ref.shape[0]
    nc = t // LANES
    if n_maps == 2:
        q = q_ref[...]
        lane = lax.broadcasted_iota(jnp.int32, q.shape, 1)
        zero = jnp.zeros_like(q)
        qs = [jnp.where(lane < DF_DQK, q, zero), jnp.where(lane >= DF_DQK, q, zero)]
    else:
        qs = [q_ref[...]]
    q_all = qs[0] if n_maps == 1 else jnp.concatenate(qs, axis=0)

    m_ref[...] = jnp.full_like(m_ref, NEG_BIG)
    l_ref[...] = jnp.zeros_like(l_ref)
    acc_ref[...] = jnp.zeros_like(acc_ref)

    chains = [(mi, slice(r0, r0 + ATT_ROWS)) for mi in range(n_maps) for r0 in range(0, t, ATT_ROWS)]

    def key_rows(n):
        return pl.ds(pl.multiple_of(n * t, t), t)

    def visible(r, diag):
        return r.stop // LANES if diag else nc

    def logits(n, slot, diag=False):
        kt = k_ref[key_rows(n), :]
        if not diag:
            s = _dot_nt(q_all, kt)
            for mi in range(n_maps):
                s_ref[slot, mi] = s[mi * t:(mi + 1) * t]
            return
        for mi, r in chains:
            w = visible(r, diag) * LANES
            s_ref[slot, mi, r, :w] = _dot_nt(qs[mi][r], kt[:w])

    def consume(n, slot, bias_ref, const, diag=False):
        vt = v_ref[key_rows(n), :]
        for mi, r in chains:
            def col(c):
                x = s_ref[slot, mi, r, c * LANES:(c + 1) * LANES]
                return x if bias_ref is None else x + bias_ref[0, r, c * LANES:(c + 1) * LANES]

            n_cols = visible(r, diag)
            m_prev = m_ref[mi, r]
            tile_max = functools.reduce(jnp.maximum, [col(c) for c in range(n_cols)])
            row_max = jnp.max(tile_max, axis=-1, keepdims=True)
            if const is None:
                m_new = jnp.maximum(m_prev, row_max)
                shift = m_new
            else:
                m_new = jnp.maximum(m_prev, row_max + const)
                shift = m_new - const
            m_ref[mi, r] = m_new
            alpha = jnp.exp2(m_prev - m_new)
            ps = [jnp.exp2(col(c) - shift) for c in range(n_cols)]
            l_ref[mi, r] = alpha * l_ref[mi, r] + functools.reduce(jnp.add, ps)
            p = jnp.concatenate([x.astype(BF16) for x in ps], axis=-1)
            acc_ref[mi, r] = alpha * acc_ref[mi, r] + _dot(p, vt[:p.shape[1]])

    far = far_ref[0, 0:1, 0:1] if has_bias else None
    prev_bias = b1_ref if has_bias else None

    n_far = jnp.maximum(i - 1, 0)
    n_pairs = lax.shift_right_logical(n_far, 1)
    logits(0, 0)

    def pair_body(j, carry):
        n = 2 * j
        logits(n + 1, 1)
        consume(n, 0, None, far)
        logits(n + 2, 0)
        consume(n + 1, 1, None, far)
        return carry

    lax.fori_loop(0, n_pairs, pair_body, 0)

    @pl.when(i == 0)
    def _():
        consume(0, 0, b0_ref, None, diag=True)

    @pl.when((i >= 1) & ((n_far & 1) == 0))
    def _():
        logits(i, 1, diag=True)
        consume(i - 1, 0, prev_bias, None)
        consume(i, 1, b0_ref, None, diag=True)

    @pl.when((n_far & 1) == 1)
    def _():
        logits(i - 1, 1)
        consume(i - 2, 0, None, far)
        logits(i, 0, diag=True)
        consume(i - 1, 1, prev_bias, None)
        consume(i, 0, b0_ref, None, diag=True)

    outs = [acc_ref[mi] / jnp.sum(l_ref[mi], axis=-1, keepdims=True) for mi in range(n_maps)]
    o_ref[...] = finish(outs, *extra).astype(o_ref.dtype)


def _df_finish(outs, lam_ref, sg_ref, *, lambda_init):
    o = outs[0] - lam_ref[...] * outs[1]
    return _rms_rows(o, sg_ref[...]) * (1.0 - lambda_init)


def _ml_finish(outs):
    return outs[0]


def _df_attn(qn, kn, vb, b0, b1, far, lam, sg, batch, seq, lambda_init):
    m = qn.shape[0]
    t = ATT_TILE
    nq = seq // t
    kv = pl.BlockSpec((seq, DF_DV), lambda b, h, i: (b, h))
    tile = pl.BlockSpec((1, t, t), lambda b, h, i: (h, 0, 0))
    vec = pl.BlockSpec((1, DF_DV), lambda b, h, i: (0, 0))
    kern = functools.partial(_attn_kernel, n_maps=2, has_bias=True,
                             finish=functools.partial(_df_finish, lambda_init=lambda_init))
    return pl.pallas_call(
        kern,
        grid=(batch, DF_HEADS, nq),
        in_specs=[pl.BlockSpec((t, DF_DV), lambda b, h, i: (b * nq + i, h)), kv, kv, tile, tile,
                  pl.BlockSpec((1, 8, LANES), lambda b, h, i: (h, 0, 0)), vec, vec],
        out_specs=pl.BlockSpec((t, DF_DV), lambda b, h, i: (b * nq + i, h)),
        out_shape=jax.ShapeDtypeStruct((m, DF_W), BF16),
        scratch_shapes=[pltpu.VMEM((2, t, LANES), F32), pltpu.VMEM((2, t, LANES), F32),
                        pltpu.VMEM((2, t, DF_DV), F32), pltpu.VMEM((2, 2, t, t), F32)],
        compiler_params=_params("parallel", "parallel", "arbitrary"),
        name="df_attn",
    )(qn, kn, vb, b0, b1, far, lam, sg)


def _ml_attn(qh, kh, vb, mask, batch, seq):
    m = qh.shape[0]
    t = ATT_TILE
    nq = seq // t
    return pl.pallas_call(
        functools.partial(_attn_kernel, n_maps=1, has_bias=False, finish=_ml_finish),
        grid=(batch, ML_HEADS, nq),
        in_specs=[pl.BlockSpec((t, ML_HEAD_PAD), lambda b, h, i: (b * nq + i, h)),
                  pl.BlockSpec((seq, ML_HEAD_PAD), lambda b, h, i: (b, h)),
                  pl.BlockSpec((seq, ML_DV), lambda b, h, i: (b, h)),
                  pl.BlockSpec((1, t, t), lambda b, h, i: (0, 0, 0))],
        out_specs=pl.BlockSpec((t, ML_DV), lambda b, h, i: (b * nq + i, h)),
        out_shape=jax.ShapeDtypeStruct((m, ML_W), BF16),
        scratch_shapes=[pltpu.VMEM((1, t, LANES), F32), pltpu.VMEM((1, t, LANES), F32),
                        pltpu.VMEM((1, t, ML_DV), F32), pltpu.VMEM((2, 1, t, t), F32)],
        compiler_params=_params("parallel", "parallel", "arbitrary"),
        name="ml_attn",
    )(qh, kh, vb, mask)


def _rope_pad(x, cos, sin):
    return x * cos + pltpu.roll(x, LANES // 2, 1) * sin


def _ml_prep_kernel(cq_ref, ckv_ref, kr_ref, qln_ref, kvln_ref, wqn_ref, wqr_ref, wkn_ref, wv_ref,
                    qgn_ref, qgr_ref, kgn_ref, kgr_ref, cos_ref, sin_ref, qo_ref, ko_ref, vo_ref):
    cos = cos_ref[...]
    sin = sin_ref[...]
    cq = _rms_rows(cq_ref[...], qln_ref[...]).astype(BF16)
    ckv = _rms_rows(ckv_ref[...], kvln_ref[...]).astype(BF16)
    q_nope = _dot(cq, wqn_ref[...])
    q_rope = _dot(cq, wqr_ref[...])
    k_nope = _dot(ckv, wkn_ref[...])
    vo_ref[...] = _dot(ckv, wv_ref[...]).astype(BF16)
    k_raw = kr_ref[...]
    lane = lax.broadcasted_iota(jnp.int32, k_raw.shape, 1)
    k_lo = jnp.where(lane < ML_ROPE // 2, k_raw, 0.0)
    k_rope = k_lo + pltpu.roll(k_raw - k_lo, ML_ROPE // 2, 1)
    kr_sq = k_rope * k_rope
    kr_rot = _rope_pad(k_rope * kgr_ref[...], cos, sin)
    qgn, qgr, kgn = qgn_ref[...], qgr_ref[...], kgn_ref[...]
    for h in range(ML_HEADS):
        sl = slice(h * LANES, (h + 1) * LANES)
        qn, qr, kn = q_nope[:, sl], q_rope[:, sl], k_nope[:, sl]
        q_ss = jnp.sum(qn * qn + qr * qr, axis=-1, keepdims=True)
        q_rs = lax.rsqrt(q_ss * (1.0 / ML_DQK) + EPS) * (ML_DQK ** -0.5 * LOG2E)
        k_ss = jnp.sum(kn * kn + kr_sq, axis=-1, keepdims=True)
        k_rs = lax.rsqrt(k_ss * (1.0 / ML_DQK) + EPS)
        base = h * ML_HEAD_PAD
        qo_ref[:, base:base + LANES] = (qn * q_rs * qgn).astype(BF16)
        qo_ref[:, base + LANES:base + 2 * LANES] = (_rope_pad(qr * qgr, cos, sin) * q_rs).astype(BF16)
        ko_ref[:, base:base + LANES] = (kn * k_rs * kgn).astype(BF16)
        ko_ref[:, base + LANES:base + 2 * LANES] = (kr_rot * k_rs).astype(BF16)


def _ml_prep(proj, qln, kvln, wqn, wqr, wkn, wv, qgn, qgr, kgn, kgr, cos, sin, seq, layer):
    m = proj.shape[1]
    npos = seq // ROW_TILE

    def full(a):
        nd = a.ndim
        if nd == 3:
            return pl.BlockSpec((None,) + a.shape[1:], lambda i: (layer, 0, 0))
        return pl.BlockSpec(a.shape, lambda i: (0, 0))

    pos = pl.BlockSpec((ROW_TILE, LANES), lambda i: (i % npos, 0))
    hp = ML_HEADS * ML_HEAD_PAD
    return pl.pallas_call(
        _ml_prep_kernel,
        grid=(m // ROW_TILE,),
        in_specs=[pl.BlockSpec((None, ROW_TILE, ML_Q_RANK), lambda i: (_MLQ_TILE, i, 0)),
                  pl.BlockSpec((None, ROW_TILE, ML_KV_RANK), lambda i: (_MLKV_TILE, i, 0)),
                  pl.BlockSpec((None, ROW_TILE, LANES), lambda i: (_MLKV_TILE, i, _MLR_LANE_BLOCK)),
                  full(qln), full(kvln), full(wqn), full(wqr), full(wkn), full(wv),
                  full(qgn), full(qgr), full(kgn), full(kgr), pos, pos],
        out_specs=[pl.BlockSpec((ROW_TILE, hp), lambda i: (i, 0)),
                   pl.BlockSpec((ROW_TILE, hp), lambda i: (i, 0)),
                   pl.BlockSpec((ROW_TILE, ML_W), lambda i: (i, 0))],
        out_shape=[jax.ShapeDtypeStruct((m, hp), BF16), jax.ShapeDtypeStruct((m, hp), BF16),
                   jax.ShapeDtypeStruct((m, ML_W), BF16)],
        compiler_params=_params("parallel"),
        name="ml_prep",
    )(proj, proj, proj, qln, kvln, wqn, wqr, wkn, wv, qgn, qgr, kgn, kgr, cos, sin)


def _out_kernel(x_ref, hg_ref, df_ref, ml_ref, w1_ref, w2_ref, w3_ref, o_ref):
    o_ref[...] = (x_ref[...] + _dot(hg_ref[...], w1_ref[...]) + _dot(df_ref[...], w2_ref[...])
                  + _dot(ml_ref[...], w3_ref[...]))


def _out_proj(x, o_hg, o_df, o_ml, w1, w2, w3, layer):
    m, d = x.shape

    def rows(w):
        return pl.BlockSpec((ROW_TILE, w), lambda i: (i, 0))

    def wspec(w):
        return pl.BlockSpec((None,) + w.shape[1:], lambda i: (layer, 0, 0))

    return pl.pallas_call(
        _out_kernel,
        grid=(m // ROW_TILE,),
        in_specs=[rows(d), rows(HG_W), rows(DF_W), rows(ML_W), wspec(w1), wspec(w2), wspec(w3)],
        out_specs=rows(d),
        out_shape=jax.ShapeDtypeStruct((m, d), F32),
        compiler_params=_params("parallel"),
        name="out_proj",
    )(x, o_hg, o_df, o_ml, w1, w2, w3)


def _t5_bucket(rel):
    half = REL_BUCKETS // 2
    max_exact = half // 2
    ret = (rel > 0).astype(jnp.int32) * half
    n = jnp.abs(rel)
    large = max_exact + (jnp.log(jnp.maximum(n, 1).astype(F32) / max_exact)
                         / math.log(REL_MAX_DIST / max_exact) * (half - max_exact)).astype(jnp.int32)
    large = jnp.minimum(large, half - 1)
    return ret + jnp.where(n < max_exact, n, large)


def _chunk_mask_tile(t):
    ii = jnp.arange(t)
    return (ii[None, :] // CHUNK) <= (ii[:, None] // CHUNK)


def _bias_tiles(rel_bias, t):
    assert t >= REL_MAX_DIST and t % CHUNK == 0
    table = rel_bias.astype(F32)

    def lookup(bucket):
        out = jnp.zeros((DF_HEADS,) + bucket.shape, F32)
        for b in range(REL_BUCKETS):
            out = out + jnp.where(bucket[None] == b, table[b][:, None, None], 0.0)
        return out

    ii = jnp.arange(t)
    rel = ii[None, :] - ii[:, None]
    b0 = jnp.where(_chunk_mask_tile(t)[None], lookup(_t5_bucket(rel)) * LOG2E, NEG_BIG)
    b1 = lookup(_t5_bucket(rel - t)) * LOG2E
    far = table[_t5_bucket(jnp.full((), -2 * t, jnp.int32))] * LOG2E
    far = jnp.broadcast_to(far[:, None, None], (DF_HEADS, 8, LANES))
    return b0, b1, far


def _rope_tables(seq):
    r = ML_ROPE
    freqs = ROPE_BASE ** (-jnp.arange(0, r, 2, dtype=F32) / r)
    ang = jnp.arange(seq).astype(F32)[:, None] * freqs[None, :]
    cos, sin = jnp.cos(ang), jnp.sin(ang)
    return (_spread_rope(jnp.concatenate([cos, cos], axis=-1)),
            _spread_rope(jnp.concatenate([-sin, sin], axis=-1)))


def _spread_rope(a):
    half = ML_ROPE // 2
    z = jnp.zeros(a.shape[:-1] + (LANES // 2 - half,), a.dtype)
    return jnp.concatenate([a[..., :half], z, a[..., half:], z], axis=-1)


def _pad_axis(a, n, axis):
    shape = list(a.shape)
    shape[axis] = n - a.shape[axis]
    return jnp.concatenate([a, jnp.zeros(shape, a.dtype)], axis=axis)


def _pad_last(a, n):
    return _pad_axis(a, n, a.ndim - 1)


def _spread_rope_heads(a):
    lead = a.shape[:-1]
    return _spread_rope(a.reshape(lead + (ML_HEADS, ML_ROPE))).reshape(lead + (ML_HEADS * LANES,))


def kernel(x, ffn_a_norm, ffn_a_w_gate, ffn_a_w_up, ffn_a_w_down, mix_norm, w_in, w_out, hgrn_lb_logits, hgrn_out_norm, diff_q_norm, diff_k_norm, diff_lambda_q1, diff_lambda_k1, diff_lambda_q2, diff_lambda_k2, diff_subln, rel_bias, mla_q_lora_norm, mla_w_uq, mla_kv_lora_norm, mla_w_ukv, mla_q_norm, mla_k_norm, ffn_b_norm, ffn_b_w_gate, ffn_b_w_up, ffn_b_w_down):
    batch, seq, d = x.shape
    depth = w_in.shape[0]
    m = batch * seq
    assert seq % ROW_TILE == 0 and seq % ATT_TILE == 0 and seq % HG_ROWS == 0

    def ffn_weights(wg, wu, wd):
        return (_cast_pad_cols(wg, D_FF_PAD), _cast_pad_cols(wu, D_FF_PAD),
                _pad_axis(wd.astype(BF16), D_FF_PAD, 1))

    wa = ffn_weights(ffn_a_w_gate, ffn_a_w_up, ffn_a_w_down)
    wb = ffn_weights(ffn_b_w_gate, ffn_b_w_up, ffn_b_w_down)
    w_in_p = _pad_last(w_in.astype(BF16), P_IN_PAD)
    w_o = w_out.astype(BF16)
    w_o1, w_o2, w_o3 = w_o[:, :HG_W], w_o[:, HG_W:HG_W + DF_W], w_o[:, HG_W + DF_W:]

    uq = mla_w_uq.reshape(depth, ML_Q_RANK, ML_HEADS, ML_DQK)
    wqn = uq[..., :ML_NOPE].reshape(depth, ML_Q_RANK, ML_HEADS * ML_NOPE).astype(BF16)
    wqr = _spread_rope_heads(uq[..., ML_NOPE:].reshape(depth, ML_Q_RANK, ML_HEADS * ML_ROPE)).astype(BF16)
    ukv = mla_w_ukv.reshape(depth, ML_KV_RANK, ML_HEADS, ML_NOPE + ML_DV)
    wkn = ukv[..., :ML_NOPE].reshape(depth, ML_KV_RANK, ML_HEADS * ML_NOPE).astype(BF16)
    wv = ukv[..., ML_NOPE:].reshape(depth, ML_KV_RANK, ML_HEADS * ML_DV).astype(BF16)

    lb_all = jnp.cumsum(jax.nn.softmax(hgrn_lb_logits.astype(F32), axis=0), axis=0)
    lb_all = lb_all - lb_all[0:1]

    hg_masks = jnp.asarray(_hgrn_level_masks())
    hg_row_masks = jnp.asarray(_hgrn_row_masks())
    b0, b1, far = _bias_tiles(rel_bias, ATT_TILE)
    ml_mask = jnp.where(_chunk_mask_tile(ATT_TILE), 0.0, NEG_BIG).astype(F32)[None]
    cos, sin = _rope_tables(seq)
    ones_blk = jnp.asarray(np.kron(np.eye(DF_W // DF_DQK), np.ones((DF_DQK, DF_DQK))), BF16)

    def row3(a):
        return a.reshape(a.shape[0], 1, a.shape[1])

    g_a, g_mix, g_b = row3(ffn_a_norm), row3(mix_norm), row3(ffn_b_norm)
    g_qln, g_kvln = row3(mla_q_lora_norm), row3(mla_kv_lora_norm)

    xf = x.reshape(m, d)
    for l in range(depth):
        xf = _ffn(xf, g_a, *wa, l)

        proj_hg = _proj(xf, g_mix, w_in_p, l, 0, _HG_TILES, HG_W)
        proj = _proj(xf, g_mix, w_in_p, l, _DM_FIRST, _DM_TILES, DF_W)
        o_hg = _hgrn(proj_hg, lb_all[l][None, :], hgrn_out_norm[l][None, :], hg_masks, hg_row_masks,
                     batch, seq)

        lambda_init = 0.8 - 0.6 * math.exp(-0.3 * l)
        lam = (jnp.exp(jnp.sum(diff_lambda_q1[l].astype(F32) * diff_lambda_k1[l].astype(F32)))
               - jnp.exp(jnp.sum(diff_lambda_q2[l].astype(F32) * diff_lambda_k2[l].astype(F32)))
               + lambda_init)
        qn, kn, vb = _df_prep(proj, jnp.tile(diff_q_norm[l], DF_W // DF_DQK)[None, :],
                              jnp.tile(diff_k_norm[l], DF_W // DF_DQK)[None, :], ones_blk)
        o_df = _df_attn(qn, kn, vb, b0, b1, far, jnp.full((1, DF_DV), lam, F32),
                        diff_subln[l][None, :], batch, seq, lambda_init)

        qg, kg = mla_q_norm[l], mla_k_norm[l]
        qh, kh, vm = _ml_prep(proj, g_qln, g_kvln, wqn, wqr, wkn, wv,
                              qg[None, :ML_NOPE], _spread_rope(qg[None, ML_NOPE:]),
                              kg[None, :ML_NOPE], _spread_rope(kg[None, ML_NOPE:]),
                              cos, sin, seq, l)
        o_ml = _ml_attn(qh, kh, vm, ml_mask, batch, seq)

        xf = _out_proj(xf, o_hg, o_df, o_ml, w_o1, w_o2, w_o3, l)
        xf = _ffn(xf, g_b, *wb, l)
    return xf.reshape(batch, seq, d)
```
